```python
import math
import jax, jax.numpy as jnp
from jax import lax
import numpy as np

D_MODEL = 1024
BATCH = 32
SEQ = 2048
DEPTH = 2
DEC_BATCH = 16
DEC_SEQ = 2048
PAST_LEN = 128

M_HEADS = 4
D_M = D_MODEL
M_DH = D_M // M_HEADS
M_CHUNK = 64
CONV_K = 5
FORGET_BIAS = 3.0
A_HEADS = 8
A_KV_HEADS = 2
A_DH = 64
D_A = A_HEADS * A_DH
A_KV = A_KV_HEADS * A_DH
WINDOW = 128
A_BLOCK = WINDOW
D_C = D_MODEL // 2
C_GROUPS = 4
C_CHUNK = 128
D_FF = 4 * D_MODEL
N_BRANCH = 3
ALPHA = (2 * DEPTH) ** 0.25
BETA = (8 * DEPTH) ** -0.25
LN_EPS = 1e-5

OFF_MX = 0
OFF_MZ = OFF_MX + D_M
OFF_MG = OFF_MZ + D_M
OFF_AQ = OFF_MG + 4 * M_HEADS
OFF_AK = OFF_AQ + D_A
OFF_AV = OFF_AK + A_KV
OFF_C = OFF_AV + A_KV
OFF_G = OFF_C + 2 * D_C
N_IN = OFF_G + N_BRANCH * D_MODEL

kernel_name = 'hybrid_bidir_mlstm_swa_sgu_encoder'


def layer_norm(x, w, b):
    xf = x.astype(jnp.float32)
    mu = jnp.mean(xf, -1, keepdims=True)
    var = jnp.mean(jnp.square(xf - mu), -1, keepdims=True)
    return ((xf - mu) * lax.rsqrt(var + LN_EPS)).astype(x.dtype) * w + b


def centred_dwconv(x, w, b):
    pad = CONV_K // 2
    y = lax.conv_general_dilated(x, w[:, None, :], window_strides=(1,), padding=[(pad, pad)],
                                 dimension_numbers=('NWC', 'WIO', 'NWC'),
                                 feature_group_count=x.shape[-1])
    return y + b


def mlstm_scan(q, k, v, ig, fg):
    B, H, S, dh = q.shape
    L = M_CHUNK
    nC = S // L
    logf = jax.nn.log_sigmoid(fg)

    def to_chunks(t):
        return jnp.moveaxis(t.reshape(B, H, nC, L, *t.shape[3:]), 2, 0)

    xs = (to_chunks(q), to_chunks(k), to_chunks(v), to_chunks(ig), to_chunks(logf))
    lower = jnp.tril(jnp.ones((L, L), dtype=bool))

    def step(carry, inp):
        C, n, m = carry
        qj, kj, vj, ij, fj = inp
        a = jnp.cumsum(fj, axis=-1)
        A = a[..., -1]
        Dm = a[..., :, None] - a[..., None, :] + ij[..., None, :]
        Dm = jnp.where(lower, Dm, -jnp.inf)
        inter = a + m[..., None]
        m_row = jnp.maximum(inter, jnp.max(Dm, -1))
        w_intra = jnp.exp(Dm - m_row[..., None])
        w_inter = jnp.exp(inter - m_row)
        s = jnp.einsum('bhld,bhsd->bhls', qj, kj) * w_intra
        num = (jnp.einsum('bhls,bhsd->bhld', s, vj)
               + w_inter[..., None] * jnp.einsum('bhvk,bhlk->bhlv', C, qj))
        den = jnp.sum(s, -1) + w_inter * jnp.einsum('bhk,bhlk->bhl', n, qj)
        h = num / jnp.maximum(jnp.abs(den), jnp.exp(-m_row))[..., None]
        g = A[..., None] - a + ij
        m_new = jnp.maximum(A + m, jnp.max(g, -1))
        wg = jnp.exp(g - m_new[..., None])
        decay = jnp.exp(A + m - m_new)
        C_new = decay[..., None, None] * C + jnp.einsum('bhs,bhsv,bhsk->bhvk', wg, vj, kj)
        n_new = decay[..., None] * n + jnp.einsum('bhs,bhsk->bhk', wg, kj)
        return (C_new, n_new, m_new), h

    init = (jnp.zeros((B, H, dh, dh), jnp.float32), jnp.zeros((B, H, dh), jnp.float32),
            jnp.zeros((B, H), jnp.float32))
    _, hc = lax.scan(step, init, xs)
    return jnp.moveaxis(hc, 0, 2).reshape(B, H, S, dh)


def mlstm_branch(xm, zm, gates, conv_w, conv_b, wq, wk, wv, norm_w):
    B, S, _ = xm.shape
    xc = jax.nn.silu(centred_dwconv(xm, conv_w, conv_b))
    heads = lambda t: t.reshape(B, S, M_HEADS, M_DH)
    f32 = jnp.float32
    q = jnp.einsum('bshd,hde->bhse', heads(xc), wq).astype(f32)
    k = (jnp.einsum('bshd,hde->bhse', heads(xc), wk) * (M_DH ** -0.5)).astype(f32)
    v = jnp.einsum('bshd,hde->bhse', heads(xm), wv).astype(f32)
    g = jnp.moveaxis(gates.astype(f32).reshape(B, S, 4, M_HEADS), 1, -1)
    h_fwd = mlstm_scan(q, k, v, g[:, 0], g[:, 1])
    rev = lambda t: jnp.flip(t, axis=2)
    h_bwd = rev(mlstm_scan(rev(q), rev(k), rev(v), rev(g[:, 2]), rev(g[:, 3])))
    h = h_fwd + h_bwd
    mu = jnp.mean(h, -1, keepdims=True)
    var = jnp.mean(jnp.square(h - mu), -1, keepdims=True)
    hn = (h - mu) * lax.rsqrt(var + LN_EPS)
    hn = jnp.moveaxis(hn, 1, 2).reshape(B, S, D_M).astype(xm.dtype) * norm_w
    return hn * jax.nn.sigmoid(zm)


def window_attention(q, k, v, sink):
    B, S = q.shape[0], q.shape[1]
    Q = A_BLOCK
    nB = S // Q
    G, R = A_KV_HEADS, A_HEADS // A_KV_HEADS
    f32 = jnp.float32
    qb = q.reshape(B, nB, Q, G, R, A_DH)
    pad = ((0, 0), (Q, Q), (0, 0), (0, 0))
    kp = jnp.pad(k, pad)
    vp = jnp.pad(v, pad)
    idx = jnp.arange(nB)[:, None] * Q + jnp.arange(3 * Q)[None, :]
    kb = kp[:, idx]
    vb = vp[:, idx]
    s = jnp.einsum('bnqgrd,bnkgd->bgrnqk', qb, kb).astype(f32) * (A_DH ** -0.5)
    qpos = jnp.arange(S).reshape(nB, Q)
    kpos = idx - Q
    dist_i = jnp.abs(qpos[:, :, None] - kpos[:, None, :])
    valid = (dist_i <= WINDOW) & (kpos[:, None, :] >= 0) & (kpos[:, None, :] < S)
    slopes = jnp.exp2(-8.0 * (jnp.arange(A_HEADS, dtype=f32) + 1.0) / A_HEADS).reshape(G, R)
    s = s - slopes[:, :, None, None, None] * dist_i.astype(f32)
    s = jnp.where(valid, s, -jnp.inf)
    sink_f = sink.astype(f32).reshape(G, R)[:, :, None, None, None]
    mx = jnp.maximum(jnp.max(s, -1, keepdims=True), sink_f)
    p = jnp.exp(s - mx)
    p = p / (jnp.sum(p, -1, keepdims=True) + jnp.exp(sink_f - mx))
    o = jnp.einsum('bgrnqk,bnkgd->bnqgrd', p.astype(v.dtype), vb)
    return o.reshape(B, S, D_A)


def spatial_gating(uv, ln_w, ln_b, ws, bs):
    B, S, _ = uv.shape
    uv = jax.nn.gelu(uv)
    u = uv[..., :D_C]
    v = layer_norm(uv[..., D_C:], ln_w, ln_b)
    nC = S // C_CHUNK
    vg = v.reshape(B, nC, C_CHUNK, C_GROUPS, D_C // C_GROUPS)
    vs = jnp.einsum('gts,bnsgc->bntgc', ws, vg) + bs.T[:, :, None]
    return u * vs.reshape(B, S, D_C)


def token_mixers(h, p, l):
    B, S, _ = h.shape
    proj = h @ p['w_in'][l] + p['b_in'][l]
    y_m = mlstm_branch(proj[..., OFF_MX:OFF_MZ], proj[..., OFF_MZ:OFF_MG], proj[..., OFF_MG:OFF_AQ],
                       p['m_conv_w'][l], p['m_conv_b'][l], p['m_wq'][l], p['m_wk'][l],
                       p['m_wv'][l], p['m_norm_w'][l])
    q = proj[..., OFF_AQ:OFF_AK].reshape(B, S, A_HEADS, A_DH)
    k = proj[..., OFF_AK:OFF_AV].reshape(B, S, A_KV_HEADS, A_DH)
    v = proj[..., OFF_AV:OFF_C].reshape(B, S, A_KV_HEADS, A_DH)
    y_a = window_attention(q, k, v, p['a_sink'][l])
    y_c = spatial_gating(proj[..., OFF_C:OFF_G], p['c_ln_w'][l], p['c_ln_b'][l],
                         p['c_ws'][l], p['c_bs'][l])
    g = jax.nn.sigmoid(proj[..., OFF_G:]).reshape(B, S, N_BRANCH, D_MODEL)
    merged = (g[..., 0, :] * (y_m @ p['p_m'][l])
              + g[..., 1, :] * (y_a @ p['p_a'][l])
              + g[..., 2, :] * (y_c @ p['p_c'][l]))
    return merged @ p['w_out'][l]


def encoder_trunk(x, c, p):
    for l in range(DEPTH):
        mod = jax.nn.silu(c) @ p['ada_w'][l] + p['ada_b'][l]
        sh1, sc1, g1, sh2, sc2, g2 = jnp.split(mod[:, None, :], 6, axis=-1)
        mix = token_mixers(x * (1 + sc1) + sh1, p, l)
        x = layer_norm(ALPHA * x + (1 + g1) * mix, p['ln1_w'][l], p['ln1_b'][l])
        hid = jnp.square(jax.nn.relu((x * (1 + sc2) + sh2) @ p['mlp_w1'][l] + p['mlp_b1'][l]))
        ff = hid @ p['mlp_w2'][l] + p['mlp_b2'][l]
        x = layer_norm(ALPHA * x + (1 + g2) * ff, p['ln2_w'][l], p['ln2_b'][l])
    return x


def setup_inputs(seed: int = 0) -> dict:
    key = jax.random.key(seed)
    ks = iter(jax.random.split(key, 40))
    nrm = lambda shape, scale: jax.random.normal(next(ks), shape, jnp.float32) * scale
    L = DEPTH
    f_cols = np.concatenate([OFF_MG + M_HEADS + np.arange(M_HEADS),
                             OFF_MG + 3 * M_HEADS + np.arange(M_HEADS)])
    b_in = nrm((L, N_IN), 0.02).at[:, f_cols].add(FORGET_BIAS)
    return {
        'x_prompt': nrm((BATCH, SEQ, D_MODEL), 1.0),
        'x_sample': nrm((DEC_BATCH, DEC_SEQ, D_MODEL), 1.0),
        'c_prompt': nrm((BATCH, D_MODEL), 1.0),
        'c_sample': nrm((DEC_BATCH, D_MODEL), 1.0),
        'ada_w': nrm((L, D_MODEL, 6 * D_MODEL), 0.2 * D_MODEL ** -0.5),
        'ada_b': nrm((L, 6 * D_MODEL), 0.02),
        'w_in': nrm((L, D_MODEL, N_IN), D_MODEL ** -0.5),
        'b_in': b_in,
        'm_conv_w': nrm((L, CONV_K, D_M), CONV_K ** -0.5),
        'm_conv_b': nrm((L, D_M), 0.02),
        'm_wq': nrm((L, M_HEADS, M_DH, M_DH), M_DH ** -0.5),
        'm_wk': nrm((L, M_HEADS, M_DH, M_DH), M_DH ** -0.5),
        'm_wv': nrm((L, M_HEADS, M_DH, M_DH), M_DH ** -0.5),
        'm_norm_w': 1.0 + nrm((L, D_M), 0.02),
        'a_sink': nrm((L, A_HEADS), 0.5),
        'c_ln_w': 1.0 + nrm((L, D_C), 0.02),
        'c_ln_b': nrm((L, D_C), 0.02),
        'c_ws': nrm((L, C_GROUPS, C_CHUNK, C_CHUNK), C_CHUNK ** -0.5),
        'c_bs': 1.0 + nrm((L, C_GROUPS, C_CHUNK), 0.02),
        'p_m': nrm((L, D_M, D_MODEL), BETA * D_M ** -0.5),
        'p_a': nrm((L, D_A, D_MODEL), BETA * D_A ** -0.5),
        'p_c': nrm((L, D_C, D_MODEL), BETA * D_C ** -0.5),
        'w_out': nrm((L, D_MODEL, D_MODEL), BETA * D_MODEL ** -0.5),
        'ln1_w': 1.0 + nrm((L, D_MODEL), 0.02),
        'ln1_b': nrm((L, D_MODEL), 0.02),
        'mlp_w1': nrm((L, D_MODEL, D_FF), D_MODEL ** -0.5),
        'mlp_b1': nrm((L, D_FF), 0.02),
        'mlp_w2': nrm((L, D_FF, D_MODEL), BETA * D_FF ** -0.5),
        'mlp_b2': nrm((L, D_MODEL), 0.02),
        'ln2_w': 1.0 + nrm((L, D_MODEL), 0.02),
        'ln2_b': nrm((L, D_MODEL), 0.02),
    }


def reference(x_prompt, x_sample, c_prompt, c_sample, ada_w, ada_b, w_in, b_in, m_conv_w, m_conv_b,
              m_wq, m_wk, m_wv, m_norm_w, a_sink, c_ln_w, c_ln_b, c_ws, c_bs, p_m, p_a, p_c, w_out,
              ln1_w, ln1_b, mlp_w1, mlp_b1, mlp_w2, mlp_b2, ln2_w, ln2_b):
    params = dict(ada_w=ada_w, ada_b=ada_b, w_in=w_in, b_in=b_in, m_conv_w=m_conv_w,
                  m_conv_b=m_conv_b, m_wq=m_wq, m_wk=m_wk, m_wv=m_wv, m_norm_w=m_norm_w,
                  a_sink=a_sink, c_ln_w=c_ln_w, c_ln_b=c_ln_b, c_ws=c_ws, c_bs=c_bs,
                  p_m=p_m, p_a=p_a, p_c=p_c, w_out=w_out, ln1_w=ln1_w, ln1_b=ln1_b,
                  mlp_w1=mlp_w1, mlp_b1=mlp_b1, mlp_w2=mlp_w2, mlp_b2=mlp_b2,
                  ln2_w=ln2_w, ln2_b=ln2_b)
    y_prompt = encoder_trunk(x_prompt, c_prompt, params)
    y_sample = encoder_trunk(x_sample, c_sample, params)
    return (y_prompt, y_sample)
```

```python
import functools

import jax
import jax.numpy as jnp
from jax import lax
from jax.experimental import pallas as pl
from jax.experimental.pallas import tpu as pltpu

F32 = jnp.float32
BF16 = jnp.bfloat16

D_MODEL = 1024
M_HEADS = 4
M_DH = 256
CONV_K = 5
A_HEADS = 8
A_KV_HEADS = 2
A_DH = 64
A_REP = A_HEADS // A_KV_HEADS
D_A = A_HEADS * A_DH
A_KV = A_KV_HEADS * A_DH
WINDOW = 128
D_C = 512
C_GROUPS = 4
C_CHUNK = 128
D_FF = 4096
N_BRANCH = 3
DEPTH = 2
ALPHA = (2 * DEPTH) ** 0.25
LN_EPS = 1e-5

OFF_MX = 0
OFF_MZ = OFF_MX + D_MODEL
OFF_MG = OFF_MZ + D_MODEL
OFF_AQ = OFF_MG + 4 * M_HEADS
OFF_AK = OFF_AQ + D_A
OFF_AV = OFF_AK + A_KV
OFF_C = OFF_AV + A_KV
OFF_G = OFF_C + 2 * D_C

V7X_LANES = 128
V7X_SUBLANES_F32 = 8
V7X_VMEM_LIMIT_BYTES = 56 * 1024 * 1024

SCAN_CHUNK = 128
TOKEN_TILE = 512
HALO = V7X_SUBLANES_F32
FF_CHUNK = 1024
GATE_PAD = V7X_LANES
N_PRE = D_MODEL + 2 * A_KV + GATE_PAD
N_MIX = D_MODEL + D_A + 2 * D_C + N_BRANCH * D_MODEL
MIX_ZM = 0
MIX_Q = MIX_ZM + D_MODEL
MIX_UV = MIX_Q + D_A
MIX_G = MIX_UV + 2 * D_C


def _dot(a, b):
    return jnp.dot(a, b, preferred_element_type=F32)


def _dot_nt(a, b):
    return lax.dot_general(a, b, (((1,), (1,)), ((), ())), preferred_element_type=F32)


def _dot_tn(a, b):
    return lax.dot_general(a, b, (((0,), (0,)), ((), ())), preferred_element_type=F32)


def _sigmoid(x):
    return 1.0 / (1.0 + jnp.exp(-x))


def _log_sigmoid(x):
    return jnp.minimum(x, 0.0) - jnp.log(1.0 + jnp.exp(-jnp.abs(x)))


def _layer_norm(x, w, b):
    mu = jnp.mean(x, axis=-1, keepdims=True)
    xc = x - mu
    var = jnp.mean(xc * xc, axis=-1, keepdims=True)
    return xc * lax.rsqrt(var + LN_EPS) * w + b


def _const_spec(shape):
    nd = len(shape)
    return pl.BlockSpec(shape, lambda *_: (0,) * nd, pipeline_mode=pl.Buffered(1))


def _params(n_axes):
    return pltpu.CompilerParams(dimension_semantics=("arbitrary",) * n_axes,
                                vmem_limit_bytes=V7X_VMEM_LIMIT_BYTES)


def _mod_kernel(c_ref, w_ref, b_ref, o_ref):
    c = c_ref[...]
    s = (c * _sigmoid(c)).astype(BF16)
    o_ref[0] = _dot(s, w_ref[0]) + b_ref[0]


def _modulation(c, ada_w, ada_b):
    n_layers, _, n_out = ada_w.shape
    bsz = c.shape[0]
    tn = 1536
    return pl.pallas_call(
        _mod_kernel,
        out_shape=jax.ShapeDtypeStruct((n_layers, bsz, n_out), F32),
        grid=(n_layers, n_out // tn),
        in_specs=[pl.BlockSpec((bsz, D_MODEL), lambda l, j: (0, 0)),
                  pl.BlockSpec((1, D_MODEL, tn), lambda l, j: (l, 0, j)),
                  pl.BlockSpec((1, 1, tn), lambda l, j: (l, 0, j))],
        out_specs=pl.BlockSpec((1, bsz, tn), lambda l, j: (l, 0, j)),
        compiler_params=_params(2),
        name="adaln_mod",
    )(c, ada_w.astype(BF16), ada_b.reshape(n_layers, 1, n_out))


def _pre_kernel(xp_ref, x_ref, xn_ref, sh_ref, sc_ref, w_ref, b_ref, cw_ref, cb_ref, wq_ref, wk_ref, wv_ref,
                q_ref, k_ref, v_ref, g_ref, akv_ref, xs_scr, xm_scr):
    t = x_ref.shape[1]
    i = pl.program_id(1)
    last = pl.num_programs(1) - 1
    xs_scr[0:HALO, :] = xp_ref[0]
    xs_scr[HALO:HALO + t, :] = x_ref[0]
    xs_scr[HALO + t:, :] = xn_ref[0]
    h = (xs_scr[...] * (1.0 + sc_ref[0]) + sh_ref[0]).astype(BF16)
    proj = _dot(h, w_ref[...]) + b_ref[...]
    row = lax.broadcasted_iota(jnp.int32, (t + 2 * HALO, 1), 0)
    lo = jnp.where(i > 0, 0, HALO)
    hi = jnp.where(i < last, t + 2 * HALO, t + HALO)
    inside = jnp.logical_and(row >= lo, row < hi)
    xm_scr[...] = jnp.where(inside, proj[:, :D_MODEL], 0.0)
    main = proj[HALO:HALO + t]
    akv_ref[0] = main[:, D_MODEL:D_MODEL + 2 * A_KV].astype(BF16)
    g_ref[0] = main[:, D_MODEL + 2 * A_KV:D_MODEL + 2 * A_KV + 4 * M_HEADS]
    conv = cb_ref[...]
    for j in range(CONV_K):
        off = HALO - CONV_K // 2 + j
        conv = conv + cw_ref[j:j + 1, :] * xm_scr[off:off + t, :]
    xc = (conv * _sigmoid(conv)).astype(BF16)
    xm = xm_scr[HALO:HALO + t, :].astype(BF16)
    for hd in range(M_HEADS):
        sl = slice(hd * M_DH, (hd + 1) * M_DH)
        q_ref[0, :, sl] = _dot(xc[:, sl], wq_ref[hd]).astype(BF16)
        k_ref[0, :, sl] = (_dot(xc[:, sl], wk_ref[hd]) * (M_DH ** -0.5)).astype(BF16)
        v_ref[0, :, sl] = _dot(xm[:, sl], wv_ref[hd]).astype(BF16)


def _pre_call(x, sh, sc, w_pre, b_pre, conv_w, conv_b, wq, wk, wv):
    bsz, seq, _ = x.shape
    t = min(TOKEN_TILE, seq)
    nt = seq // t
    hb = t // HALO
    nhb = seq // HALO
    tile = lambda n: pl.BlockSpec((1, t, n), lambda b, i: (b, i, 0))
    vec = pl.BlockSpec((1, 1, D_MODEL), lambda b, i: (b, 0, 0))
    return pl.pallas_call(
        _pre_kernel,
        out_shape=(jax.ShapeDtypeStruct((bsz, seq, D_MODEL), BF16),) * 3
        + (jax.ShapeDtypeStruct((bsz, seq, 4 * M_HEADS), F32), jax.ShapeDtypeStruct((bsz, seq, 2 * A_KV), BF16)),
        grid=(bsz, nt),
        in_specs=[pl.BlockSpec((1, HALO, D_MODEL), lambda b, i: (b, jnp.maximum(i * hb - 1, 0), 0)),
                  tile(D_MODEL),
                  pl.BlockSpec((1, HALO, D_MODEL), lambda b, i: (b, jnp.minimum((i + 1) * hb, nhb - 1), 0)),
                  vec, vec,
                  _const_spec((D_MODEL, N_PRE)), _const_spec((1, N_PRE)),
                  _const_spec((CONV_K, D_MODEL)), _const_spec((1, D_MODEL)),
                  _const_spec((M_HEADS, M_DH, M_DH)), _const_spec((M_HEADS, M_DH, M_DH)),
                  _const_spec((M_HEADS, M_DH, M_DH))],
        out_specs=(tile(D_MODEL), tile(D_MODEL), tile(D_MODEL), tile(4 * M_HEADS), tile(2 * A_KV)),
        scratch_shapes=[pltpu.VMEM((t + 2 * HALO, D_MODEL), F32), pltpu.VMEM((t + 2 * HALO, D_MODEL), F32)],
        compiler_params=_params(2),
        name="pre_proj_conv_qkv",
    )(x, x, x, sh, sc, w_pre, b_pre, conv_w, conv_b, wq, wk, wv)


def _scan_chunk(q, k, v, ig_row, fg_row, ig_col, fg_col, ct, n, m, tri, tri_t):
    lf_row = _log_sigmoid(fg_row)
    lf_col = _log_sigmoid(fg_col)
    a_col = jnp.sum(jnp.where(tri, lf_row, 0.0), axis=1, keepdims=True)
    a_row = jnp.sum(jnp.where(tri_t, lf_col, 0.0), axis=0, keepdims=True)
    a_tot = jnp.sum(lf_row, axis=1, keepdims=True)
    dm = jnp.where(tri, a_col - a_row + ig_row, -jnp.inf)
    inter = a_col + m
    m_row = jnp.maximum(inter, jnp.max(dm, axis=1, keepdims=True))
    w_intra = jnp.exp(dm - m_row)
    w_inter = jnp.exp(inter - m_row)
    s = _dot_nt(q, k) * w_intra
    num = _dot(s.astype(BF16), v) + w_inter * _dot(q, ct.astype(BF16))
    qn = jnp.sum(q.astype(F32) * n, axis=1, keepdims=True)
    den = jnp.sum(s, axis=1, keepdims=True) + w_inter * qn
    h = num / jnp.maximum(jnp.abs(den), jnp.exp(-m_row))
    g_col = a_tot - a_col + ig_col
    m_new = jnp.maximum(a_tot + m, jnp.max(g_col, axis=0, keepdims=True))
    wg = jnp.exp(g_col - m_new)
    decay = jnp.exp(a_tot + m - m_new)
    ct_new = decay * ct + _dot_tn(k, (wg * v.astype(F32)).astype(BF16))
    n_new = decay * n + jnp.sum(wg * k.astype(F32), axis=0, keepdims=True)
    return h, ct_new, n_new, m_new


def _scan_kernel(q_ref, k_ref, v_ref, gc_ref, gr_ref, o_ref, hs_scr, ct_scr):
    seq = q_ref.shape[1]
    lc = gc_ref.shape[3]
    nc = seq // lc
    row = lax.broadcasted_iota(jnp.int32, (lc, lc), 0)
    col = lax.broadcasted_iota(jnp.int32, (lc, lc), 1)
    lower = col <= row
    upper = col >= row
    ct_scr[...] = jnp.zeros_like(ct_scr)

    def run_dir(d, j, n, m, accumulate):
        jc = j if d == 0 else nc - 1 - j
        r0 = pl.multiple_of(jc * lc, lc)
        rows = pl.ds(r0, lc)
        tri, tri_t = (lower, upper) if d == 0 else (upper, lower)
        ig_row = gr_ref[0, 0, 2 * d, pl.ds(jc, 1), :]
        fg_row = gr_ref[0, 0, 2 * d + 1, pl.ds(jc, 1), :]
        ig_col = gc_ref[0, 0, jc, :, 2 * d:2 * d + 1]
        fg_col = gc_ref[0, 0, jc, :, 2 * d + 1:2 * d + 2]
        h, ct_new, n_new, m_new = _scan_chunk(q_ref[0, rows, :], k_ref[0, rows, :], v_ref[0, rows, :],
                                              ig_row, fg_row, ig_col, fg_col, ct_scr[d], n, m, tri, tri_t)
        ct_scr[d] = ct_new
        if accumulate:
            hs_scr[rows, :] = hs_scr[rows, :] + h
        else:
            hs_scr[rows, :] = h
        return n_new, m_new

    def body(accumulate, j, carry):
        nf, mf, nb, mb = carry
        nf, mf = run_dir(0, j, nf, mf, accumulate)
        nb, mb = run_dir(1, j, nb, mb, accumulate)
        return nf, mf, nb, mb

    zero_n = jnp.zeros((1, M_DH), F32)
    zero_m = jnp.zeros((1, 1), F32)
    carry = (zero_n, zero_m, zero_n, zero_m)
    carry = lax.fori_loop(0, nc // 2, functools.partial(body, False), carry)
    lax.fori_loop(nc // 2, nc, functools.partial(body, True), carry)

    blk = min(256, seq)

    def norm_body(r, _):
        rows = pl.ds(pl.multiple_of(r * blk, blk), blk)
        hsum = hs_scr[rows, :]
        mu = jnp.mean(hsum, axis=-1, keepdims=True)
        hc = hsum - mu
        var = jnp.mean(hc * hc, axis=-1, keepdims=True)
        o_ref[0, rows, :] = (hc * lax.rsqrt(var + LN_EPS)).astype(o_ref.dtype)
        return 0

    lax.fori_loop(0, seq // blk, norm_body, 0)


def _scan_call(q, k, v, gcol, grow):
    bsz, seq, _ = q.shape
    nc, lc = gcol.shape[2], gcol.shape[3]
    head = pl.BlockSpec((1, seq, M_DH), lambda b, h: (b, 0, h))
    return pl.pallas_call(
        _scan_kernel,
        out_shape=jax.ShapeDtypeStruct((bsz, seq, D_MODEL), BF16),
        grid=(bsz, M_HEADS),
        in_specs=[head, head, head,
                  pl.BlockSpec((1, 1, nc, lc, 4), lambda b, h: (b, h, 0, 0, 0)),
                  pl.BlockSpec((1, 1, 4, nc, lc), lambda b, h: (b, h, 0, 0, 0))],
        out_specs=head,
        scratch_shapes=[pltpu.VMEM((seq, M_DH), F32), pltpu.VMEM((2, M_DH, M_DH), F32)],
        compiler_params=_params(2),
        name="mlstm_scan",
    )(q, k, v, gcol, grow)


def _gelu_tanh(x):
    return 0.5 * x * (1.0 + jnp.tanh(0.7978845608028654 * (x + 0.044715 * (x * x * x))))


def _mix_kernel(x_ref, hn_ref, kvp_ref, kv_ref, kvn_ref, sh_ref, sc_ref, gm_ref, sink_ref,
                w_ref, b_ref, nw_ref, clw_ref, clb_ref, ws_ref, bs_ref, pm_ref, pa_ref, pc_ref, wo_ref,
                l1w_ref, l1b_ref, o_ref, kx_scr, ya_scr, yc_scr):
    t = x_ref.shape[1]
    i = pl.program_id(1)
    seq = t * pl.num_programs(1)
    nqb = t // WINDOW
    x = x_ref[0]
    h = (x * (1.0 + sc_ref[0]) + sh_ref[0]).astype(BF16)

    def proj(off, n):
        return _dot(h, w_ref[:, off:off + n]) + b_ref[:, off:off + n]

    zm = proj(MIX_ZM, D_MODEL)
    y_m = (hn_ref[0].astype(F32) * nw_ref[...] * _sigmoid(zm)).astype(BF16)
    acc = _sigmoid(proj(MIX_G, D_MODEL)) * _dot(y_m, pm_ref[...])

    kx_scr[0:WINDOW, :] = kvp_ref[0]
    kx_scr[WINDOW:WINDOW + t, :] = kv_ref[0]
    kx_scr[WINDOW + t:, :] = kvn_ref[0]
    q = (proj(MIX_Q, D_A) * (A_DH ** -0.5)).astype(BF16)
    qi = lax.broadcasted_iota(jnp.int32, (WINDOW, 3 * WINDOW), 0)
    kj = lax.broadcasted_iota(jnp.int32, (WINDOW, 3 * WINDOW), 1)
    dist = jnp.abs(qi + WINDOW - kj)
    band = dist <= WINDOW
    distf = dist.astype(F32)
    kcol = lax.broadcasted_iota(jnp.int32, (1, 3 * WINDOW), 1)
    for g in range(A_KV_HEADS):
        bias = jnp.concatenate(
            [jnp.where(band, -(2.0 ** (-8.0 * (g * A_REP + r + 1) / A_HEADS)) * distf, -jnp.inf)
             for r in range(A_REP)], axis=0)
        sink = jnp.concatenate([jnp.full((WINDOW, 1), sink_ref[g * A_REP + r], F32) for r in range(A_REP)],
                               axis=0)
        for n in range(nqb):
            qb = jnp.concatenate([q[n * WINDOW:(n + 1) * WINDOW, (g * A_REP + r) * A_DH:(g * A_REP + r + 1) * A_DH]
                                  for r in range(A_REP)], axis=0)
            kb = kx_scr[n * WINDOW:(n + 3) * WINDOW, g * A_DH:(g + 1) * A_DH]
            vb = kx_scr[n * WINDOW:(n + 3) * WINDOW, A_KV + g * A_DH:A_KV + (g + 1) * A_DH]
            kpos = kcol + (i * t + (n - 1) * WINDOW)
            edge = jnp.logical_and(kpos >= 0, kpos < seq)
            s = jnp.where(edge, _dot_nt(qb, kb) + bias, -jnp.inf)
            mx = jnp.maximum(jnp.max(s, axis=1, keepdims=True), sink)
            p = jnp.exp(s - mx)
            den = jnp.sum(p, axis=1, keepdims=True) + jnp.exp(sink - mx)
            o = _dot(p.astype(BF16), vb) / den
            for r in range(A_REP):
                c0 = (g * A_REP + r) * A_DH
                ya_scr[n * WINDOW:(n + 1) * WINDOW, c0:c0 + A_DH] = o[r * WINDOW:(r + 1) * WINDOW]
    acc = acc + _sigmoid(proj(MIX_G + D_MODEL, D_MODEL)) * _dot(ya_scr[...].astype(BF16), pa_ref[...])

    uv = _gelu_tanh(proj(MIX_UV, 2 * D_C))
    u = uv[:, :D_C]
    vn = _layer_norm(uv[:, D_C:], clw_ref[...], clb_ref[...]).astype(BF16)
    gw = D_C // C_GROUPS
    for n in range(t // C_CHUNK):
        for g in range(C_GROUPS):
            vs = _dot(ws_ref[g], vn[n * C_CHUNK:(n + 1) * C_CHUNK, g * gw:(g + 1) * gw]) + bs_ref[g]
            yc_scr[n * C_CHUNK:(n + 1) * C_CHUNK, g * gw:(g + 1) * gw] = vs
    y_c = (u * yc_scr[...]).astype(BF16)
    acc = acc + _sigmoid(proj(MIX_G + 2 * D_MODEL, D_MODEL)) * _dot(y_c, pc_ref[...])

    mix = _dot(acc.astype(BF16), wo_ref[...])
    o_ref[0] = _layer_norm(ALPHA * x + (1.0 + gm_ref[0]) * mix, l1w_ref[...], l1b_ref[...])


def _mix_call(x, hn, akv, sh, sc, gm, sink, w_mix, b_mix, norm_w, c_ln_w, c_ln_b, c_ws, c_bs, p_m, p_a, p_c,
              w_out, ln_w, ln_b):
    bsz, seq, _ = x.shape
    t = min(TOKEN_TILE, seq)
    nt = seq // t
    wb = t // WINDOW
    nwb = seq // WINDOW
    tile = lambda n: pl.BlockSpec((1, t, n), lambda b, i: (b, i, 0))
    vec = pl.BlockSpec((1, 1, D_MODEL), lambda b, i: (b, 0, 0))
    return pl.pallas_call(
        _mix_kernel,
        out_shape=jax.ShapeDtypeStruct((bsz, seq, D_MODEL), F32),
        grid=(bsz, nt),
        in_specs=[tile(D_MODEL), tile(D_MODEL),
                  pl.BlockSpec((1, WINDOW, 2 * A_KV), lambda b, i: (b, jnp.maximum(i * wb - 1, 0), 0)),
                  tile(2 * A_KV),
                  pl.BlockSpec((1, WINDOW, 2 * A_KV), lambda b, i: (b, jnp.minimum((i + 1) * wb, nwb - 1), 0)),
                  vec, vec, vec,
                  pl.BlockSpec(memory_space=pltpu.SMEM),
                  _const_spec((D_MODEL, N_MIX)), _const_spec((1, N_MIX)),
                  _const_spec((1, D_MODEL)), _const_spec((1, D_C)), _const_spec((1, D_C)),
                  _const_spec((C_GROUPS, C_CHUNK, C_CHUNK)), _const_spec((C_GROUPS, C_CHUNK, 1)),
                  _const_spec((D_MODEL, D_MODEL)), _const_spec((D_A, D_MODEL)), _const_spec((D_C, D_MODEL)),
                  _const_spec((D_MODEL, D_MODEL)), _const_spec((1, D_MODEL)), _const_spec((1, D_MODEL))],
        out_specs=tile(D_MODEL),
        scratch_shapes=[pltpu.VMEM((t + 2 * WINDOW, 2 * A_KV), BF16), pltpu.VMEM((t, D_A), F32),
                        pltpu.VMEM((t, D_C), F32)],
        compiler_params=_params(2),
        name="mix_merge_ln1",
    )(x, hn, akv, akv, akv, sh, sc, gm, sink, w_mix, b_mix, norm_w, c_ln_w, c_ln_b, c_ws, c_bs, p_m, p_a, p_c,
      w_out, ln_w, ln_b)


def _mlp_kernel(x_ref, sh_ref, sc_ref, gm_ref, w1_ref, b1_ref, w2_ref, b2_ref, lw_ref, lb_ref, o_ref):
    x = x_ref[0]
    h = (x * (1.0 + sc_ref[0]) + sh_ref[0]).astype(BF16)
    ff = b2_ref[...]
    for c in range(D_FF // FF_CHUNK):
        cs = slice(c * FF_CHUNK, (c + 1) * FF_CHUNK)
        hid = jnp.maximum(_dot(h, w1_ref[:, cs]) + b1_ref[:, cs], 0.0)
        ff = ff + _dot((hid * hid).astype(BF16), w2_ref[cs, :])
    o_ref[0] = _layer_norm(ALPHA * x + (1.0 + gm_ref[0]) * ff, lw_ref[...], lb_ref[...])


def _mlp_call(x, sh, sc, gm, w1, b1, w2, b2, ln_w, ln_b):
    bsz, seq, _ = x.shape
    t = min(TOKEN_TILE, seq)
    tile = pl.BlockSpec((1, t, D_MODEL), lambda b, i: (b, i, 0))
    vec = pl.BlockSpec((1, 1, D_MODEL), lambda b, i: (b, 0, 0))
    return pl.pallas_call(
        _mlp_kernel,
        out_shape=jax.ShapeDtypeStruct((bsz, seq, D_MODEL), F32),
        grid=(bsz, seq // t),
        in_specs=[tile, vec, vec, vec,
                  _const_spec((D_MODEL, D_FF)), _const_spec((1, D_FF)),
                  _const_spec((D_FF, D_MODEL)), _const_spec((1, D_MODEL)),
                  _const_spec((1, D_MODEL)), _const_spec((1, D_MODEL))],
        out_specs=tile,
        compiler_params=_params(2),
        name="mlp_ln2",
    )(x, sh, sc, gm, w1, b1, w2, b2, ln_w, ln_b)


def _layer_weights(p, l):
    w_in, b_in = p["w_in"][l], p["b_in"][l]
    gate_pad = GATE_PAD - 4 * M_HEADS
    w_pre = jnp.concatenate([w_in[:, OFF_MX:OFF_MZ], w_in[:, OFF_AK:OFF_C], w_in[:, OFF_MG:OFF_AQ],
                             jnp.zeros((D_MODEL, gate_pad), F32)], axis=1).astype(BF16)
    b_pre = jnp.concatenate([b_in[OFF_MX:OFF_MZ], b_in[OFF_AK:OFF_C], b_in[OFF_MG:OFF_AQ],
                             jnp.zeros((gate_pad,), F32)])[None, :]
    w_mix = jnp.concatenate([w_in[:, OFF_MZ:OFF_MG], w_in[:, OFF_AQ:OFF_AK], w_in[:, OFF_C:]], axis=1).astype(BF16)
    b_mix = jnp.concatenate([b_in[OFF_MZ:OFF_MG], b_in[OFF_AQ:OFF_AK], b_in[OFF_C:]])[None, :]
    row = lambda name: p[name][l][None, :]
    return dict(
        w_pre=w_pre, b_pre=b_pre, w_mix=w_mix, b_mix=b_mix,
        conv_w=p["m_conv_w"][l], conv_b=row("m_conv_b"),
        wq=p["m_wq"][l].astype(BF16), wk=p["m_wk"][l].astype(BF16), wv=p["m_wv"][l].astype(BF16),
        norm_w=row("m_norm_w"), sink=p["a_sink"][l],
        c_ln_w=row("c_ln_w"), c_ln_b=row("c_ln_b"),
        c_ws=p["c_ws"][l].astype(BF16), c_bs=p["c_bs"][l][:, :, None],
        p_m=p["p_m"][l].astype(BF16), p_a=p["p_a"][l].astype(BF16), p_c=p["p_c"][l].astype(BF16),
        w_out=p["w_out"][l].astype(BF16), ln1_w=row("ln1_w"), ln1_b=row("ln1_b"),
        w1=p["mlp_w1"][l].astype(BF16), b1=row("mlp_b1"), w2=p["mlp_w2"][l].astype(BF16), b2=row("mlp_b2"),
        ln2_w=row("ln2_w"), ln2_b=row("ln2_b"))


def _trunk(x, c, p, weights):
    bsz, seq, _ = x.shape
    lc = min(SCAN_CHUNK, seq)
    nc = seq // lc
    mod = _modulation(c, p["ada_w"], p["ada_b"])
    for l in range(DEPTH):
        w = weights[l]
        sh1, sc1, g1, sh2, sc2, g2 = [mod[l, :, None, k * D_MODEL:(k + 1) * D_MODEL] for k in range(6)]
        q, k, v, gates, akv = _pre_call(x, sh1, sc1, w["w_pre"], w["b_pre"], w["conv_w"], w["conv_b"],
                                        w["wq"], w["wk"], w["wv"])
        gk = gates.reshape(bsz, nc, lc, 4, M_HEADS)
        gcol = jnp.transpose(gk, (0, 4, 1, 2, 3))
        grow = jnp.transpose(gk, (0, 4, 3, 1, 2))
        hn = _scan_call(q, k, v, gcol, grow)
        x = _mix_call(x, hn, akv, sh1, sc1, g1, w["sink"], w["w_mix"], w["b_mix"], w["norm_w"], w["c_ln_w"],
                      w["c_ln_b"], w["c_ws"], w["c_bs"], w["p_m"], w["p_a"], w["p_c"], w["w_out"],
                      w["ln1_w"], w["ln1_b"])
        x = _mlp_call(x, sh2, sc2, g2, w["w1"], w["b1"], w["w2"], w["b2"], w["ln2_w"], w["ln2_b"])
    return x


def kernel(x_prompt, x_sample, c_prompt, c_sample, ada_w, ada_b, w_in, b_in, m_conv_w, m_conv_b, m_wq, m_wk, m_wv,
           m_norm_w, a_sink, c_ln_w, c_ln_b, c_ws, c_bs, p_m, p_a, p_c, w_out, ln1_w, ln1_b, mlp_w1, mlp_b1,
           mlp_w2, mlp_b2, ln2_w, ln2_b):
    p = dict(ada_w=ada_w, ada_b=ada_b, w_in=w_in, b_in=b_in, m_conv_w=m_conv_w, m_conv_b=m_conv_b, m_wq=m_wq,
             m_wk=m_wk, m_wv=m_wv, m_norm_w=m_norm_w, a_sink=a_sink, c_ln_w=c_ln_w, c_ln_b=c_ln_b, c_ws=c_ws,
             c_bs=c_bs, p_m=p_m, p_a=p_a, p_c=p_c, w_out=w_out, ln1_w=ln1_w, ln1_b=ln1_b, mlp_w1=mlp_w1,
             mlp_b1=mlp_b1, mlp_w2=mlp_w2, mlp_b2=mlp_b2, ln2_w=ln2_w, ln2_b=ln2_b)
    weights = [_layer_weights(p, l) for l in range(DEPTH)]
    return (_trunk(x_prompt, c_prompt, p, weights), _trunk(x_sample, c_sample, p, weights))
```

```python
import jax
import jax.numpy as jnp
from jax import lax
from jax.experimental import pallas as pl
from jax.experimental.pallas import tpu as pltpu

F32 = jnp.float32
BF16 = jnp.bfloat16

D_MODEL = 1024
M_HEADS = 4
M_DH = 256
CONV_K = 5
A_HEADS = 8
A_KV_HEADS = 2
A_DH = 64
A_REP = A_HEADS // A_KV_HEADS
D_A = A_HEADS * A_DH
A_KV = A_KV_HEADS * A_DH
WINDOW = 128
D_C = 512
C_GROUPS = 4
C_CHUNK = 128
D_FF = 4096
N_BRANCH = 3
DEPTH = 2
ALPHA = (2 * DEPTH) ** 0.25
LN_EPS = 1e-5

OFF_MX = 0
OFF_MZ = OFF_MX + D_MODEL
OFF_MG = OFF_MZ + D_MODEL
OFF_AQ = OFF_MG + 4 * M_HEADS
OFF_AK = OFF_AQ + D_A
OFF_AV = OFF_AK + A_KV
OFF_C = OFF_AV + A_KV
OFF_G = OFF_C + 2 * D_C

V7X_LANES = 128
V7X_SUBLANES_F32 = 8
V7X_SUBLANES_BF16 = 16
V7X_VMEM_LIMIT_BYTES = 56 * 1024 * 1024

SCAN_CHUNK = 256
TOKEN_TILE = 512
HALO = V7X_SUBLANES_F32
FF_CHUNK = 1024
GATE_PAD = V7X_LANES
AUG = V7X_SUBLANES_BF16
N_PRE = D_MODEL + A_KV + GATE_PAD
N_MIX = D_MODEL + D_A + 2 * D_C + N_BRANCH * D_MODEL
MIX_ZM = 0
MIX_Q = MIX_ZM + D_MODEL
MIX_UV = MIX_Q + D_A
MIX_G = MIX_UV + 2 * D_C


def _dot(a, b):
    return jnp.dot(a, b, preferred_element_type=F32)


def _dot_nt(a, b):
    return lax.dot_general(a, b, (((1,), (1,)), ((), ())), preferred_element_type=F32)


def _dot_tn(a, b):
    return lax.dot_general(a, b, (((0,), (0,)), ((), ())), preferred_element_type=F32)


def _sigmoid(x):
    return 0.5 * (1.0 + jnp.tanh(0.5 * x))


def _log_sigmoid(x):
    return jnp.minimum(x, 0.0) - jnp.log(1.0 + jnp.exp(-jnp.abs(x)))


def _gelu_tanh(x):
    return 0.5 * x * (1.0 + jnp.tanh(0.7978845608028654 * (x + 0.044715 * (x * x * x))))


def _layer_norm(x, w, b):
    mu = jnp.mean(x, axis=-1, keepdims=True)
    xc = x - mu
    var = jnp.mean(xc * xc, axis=-1, keepdims=True)
    return xc * lax.rsqrt(var + LN_EPS) * w + b


def _norm_rows(x):
    mu = jnp.mean(x, axis=0, keepdims=True)
    xc = x - mu
    var = jnp.mean(xc * xc, axis=0, keepdims=True)
    return xc * lax.rsqrt(var + LN_EPS)


def _lanes(col, n):
    return jnp.concatenate([col] * (n // V7X_LANES), axis=1)


def _const_spec(shape):
    nd = len(shape)
    return pl.BlockSpec(shape, lambda *_: (0,) * nd, pipeline_mode=pl.Buffered(1))


def _params(n_axes):
    return pltpu.CompilerParams(dimension_semantics=("arbitrary",) * n_axes,
                                vmem_limit_bytes=V7X_VMEM_LIMIT_BYTES)


def _mod_kernel(c_ref, w_ref, b_ref, o_ref):
    c = c_ref[...]
    s = (c * _sigmoid(c)).astype(BF16)
    o_ref[0] = _dot(s, w_ref[0]) + b_ref[0]


def _modulation(c, ada_w, ada_b):
    n_layers, _, n_out = ada_w.shape
    bsz = c.shape[0]
    tn = 1536
    return pl.pallas_call(
        _mod_kernel,
        out_shape=jax.ShapeDtypeStruct((n_layers, bsz, n_out), F32),
        grid=(n_layers, n_out // tn),
        in_specs=[pl.BlockSpec((bsz, D_MODEL), lambda l, j: (0, 0)),
                  pl.BlockSpec((1, D_MODEL, tn), lambda l, j: (l, 0, j)),
                  pl.BlockSpec((1, 1, tn), lambda l, j: (l, 0, j))],
        out_specs=pl.BlockSpec((1, bsz, tn), lambda l, j: (l, 0, j)),
        compiler_params=_params(2),
        name="adaln_mod",
    )(c, ada_w, ada_b)


def _pre_kernel(xp_ref, x_ref, xn_ref, sh_ref, sc_ref, w_ref, b_ref, wav_ref, bav_ref, cw_ref, cb_ref,
                wqt_ref, wk_ref, wvt_ref, qt_ref, k_ref, vt_ref, g_ref, ak_ref, avt_ref, xs_scr, xm_scr):
    t = x_ref.shape[1]
    i = pl.program_id(1)
    last = pl.num_programs(1) - 1
    scale = 1.0 + sc_ref[0]
    shift = sh_ref[0]
    xs_scr[0:HALO, :] = xp_ref[0]
    xs_scr[HALO:HALO + t, :] = x_ref[0]
    xs_scr[HALO + t:, :] = xn_ref[0]
    h_ext = (xs_scr[...] * scale + shift).astype(BF16)
    proj = _dot(h_ext, w_ref[...]) + b_ref[...]
    row = lax.broadcasted_iota(jnp.int32, (t + 2 * HALO, 1), 0)
    lo = jnp.where(i > 0, 0, HALO)
    hi = jnp.where(i < last, t + 2 * HALO, t + HALO)
    inside = jnp.logical_and(row >= lo, row < hi)
    xm_scr[...] = jnp.where(inside, proj[:, :D_MODEL], 0.0)
    main = proj[HALO:HALO + t]
    ak_ref[0] = main[:, D_MODEL:D_MODEL + A_KV].astype(BF16)
    g_ref[0] = main[:, D_MODEL + A_KV:D_MODEL + A_KV + 4 * M_HEADS]
    h = (x_ref[0] * scale + shift).astype(BF16)
    avt_ref[0] = (_dot_nt(wav_ref[...], h) + _lanes(bav_ref[...], t)).astype(BF16)
    conv = cb_ref[...]
    for j in range(CONV_K):
        off = HALO - CONV_K // 2 + j
        conv = conv + cw_ref[j:j + 1, :] * xm_scr[off:off + t, :]
    xc = (conv * _sigmoid(conv)).astype(BF16)
    xm = xm_scr[HALO:HALO + t, :].astype(BF16)
    for hd in range(M_HEADS):
        sl = slice(hd * M_DH, (hd + 1) * M_DH)
        qt_ref[0, sl, :] = _dot_nt(wqt_ref[hd], xc[:, sl]).astype(BF16)
        k_ref[0, :, sl] = (_dot(xc[:, sl], wk_ref[hd]) * (M_DH ** -0.5)).astype(BF16)
        vt_ref[0, sl, :] = _dot_nt(wvt_ref[hd], xm[:, sl]).astype(BF16)


def _pre_call(x, sh, sc, w):
    bsz, seq, _ = x.shape
    t = min(TOKEN_TILE, seq)
    nt = seq // t
    hb = t // HALO
    nhb = seq // HALO
    tok = lambda n: pl.BlockSpec((1, t, n), lambda b, i: (b, i, 0))
    feat = lambda n: pl.BlockSpec((1, n, t), lambda b, i: (b, 0, i))
    vec = pl.BlockSpec((1, 1, D_MODEL), lambda b, i: (b, 0, 0))
    tok_shape = lambda n, dt: jax.ShapeDtypeStruct((bsz, seq, n), dt)
    feat_shape = lambda n: jax.ShapeDtypeStruct((bsz, n, seq), BF16)
    return pl.pallas_call(
        _pre_kernel,
        out_shape=(feat_shape(D_MODEL), tok_shape(D_MODEL, BF16), feat_shape(D_MODEL),
                   tok_shape(4 * M_HEADS, F32), tok_shape(A_KV, BF16), feat_shape(A_KV)),
        grid=(bsz, nt),
        in_specs=[pl.BlockSpec((1, HALO, D_MODEL), lambda b, i: (b, jnp.maximum(i * hb - 1, 0), 0)),
                  tok(D_MODEL),
                  pl.BlockSpec((1, HALO, D_MODEL), lambda b, i: (b, jnp.minimum((i + 1) * hb, nhb - 1), 0)),
                  vec, vec,
                  _const_spec((D_MODEL, N_PRE)), _const_spec((1, N_PRE)),
                  _const_spec((A_KV, D_MODEL)), _const_spec((A_KV, V7X_LANES)),
                  _const_spec((CONV_K, D_MODEL)), _const_spec((1, D_MODEL)),
                  _const_spec((M_HEADS, M_DH, M_DH)), _const_spec((M_HEADS, M_DH, M_DH)),
                  _const_spec((M_HEADS, M_DH, M_DH))],
        out_specs=(feat(D_MODEL), tok(D_MODEL), feat(D_MODEL), tok(4 * M_HEADS), tok(A_KV), feat(A_KV)),
        scratch_shapes=[pltpu.VMEM((t + 2 * HALO, D_MODEL), F32), pltpu.VMEM((t + 2 * HALO, D_MODEL), F32)],
        compiler_params=_params(2),
        name="pre_proj_conv_qkv",
    )(x, x, x, sh, sc, w["w_pre"], w["b_pre"], w["w_avt"], w["b_avt"], w["conv_w"], w["conv_b"],
      w["wqt"], w["wk"], w["wvt"])


def _scan_kernel(qt_ref, k_ref, vt_ref, gr_ref, o_ref, hs_scr, c_scr):
    seq = k_ref.shape[1]
    lc = gr_ref.shape[4]
    nc = seq // lc
    si = lax.broadcasted_iota(jnp.int32, (lc, lc), 0)
    li = lax.broadcasted_iota(jnp.int32, (lc, lc), 1)
    masks = (si <= li, si >= li)
    ones_rows = jnp.ones((AUG, lc), BF16)
    c_scr[...] = jnp.zeros_like(c_scr)

    gate = []
    for d in range(2):
        ig = gr_ref[0, 0, 2 * d]
        lf = _log_sigmoid(gr_ref[0, 0, 2 * d + 1])
        a = jnp.dot(lf, masks[d].astype(F32), precision=lax.Precision.HIGHEST, preferred_element_type=F32)
        w = ig - a
        a_tot = a[:, lc - 1:lc] if d == 0 else a[:, 0:1]
        w_max = jnp.max(w, axis=1, keepdims=True)
        w_cols = jnp.concatenate([w, jnp.zeros((V7X_LANES - nc, lc), F32)], axis=0).T
        m = jnp.zeros((1, 1), F32)
        m_seq = []
        for step in range(nc):
            jc = step if d == 0 else nc - 1 - step
            m_next = a_tot[jc:jc + 1] + jnp.maximum(m, w_max[jc:jc + 1])
            m_seq.append((m, m_next))
            m = m_next
        gate.append((a, w, a_tot, w_cols, m_seq))

    for step in range(nc):
        for d in range(2):
            a, w, a_tot, w_cols, m_seq = gate[d]
            jc = step if d == 0 else nc - 1 - step
            cols = slice(jc * lc, (jc + 1) * lc)
            m, m_next = m_seq[step]
            qt = qt_ref[0, :, cols]
            k = k_ref[0, cols, :]
            vta = jnp.concatenate([vt_ref[0, :, cols], ones_rows], axis=0)
            wm = jnp.where(masks[d], w_cols[:, jc:jc + 1], -jnp.inf)
            u = -jnp.maximum(m, jnp.max(wm, axis=0, keepdims=True))
            st = _dot(k, qt) * jnp.exp(wm + u)
            big = _dot(vta, st.astype(BF16)) + jnp.exp(m + u) * _dot(c_scr[d].astype(BF16), qt)
            den = jnp.maximum(jnp.abs(big[M_DH:M_DH + 1, :]), jnp.exp(u - a[jc:jc + 1, :]))
            ht = big[:M_DH, :] * (1.0 / den)
            if step < nc // 2:
                hs_scr[:, cols] = ht
            else:
                hs_scr[:, cols] = hs_scr[:, cols] + ht
            wg = jnp.exp(a_tot[jc:jc + 1] + w[jc:jc + 1, :] - m_next)
            decay = jnp.exp(a_tot[jc:jc + 1] + m - m_next)
            c_scr[d] = decay * c_scr[d] + _dot((vta.astype(F32) * wg).astype(BF16), k)

    blk = min(512, seq)
    for r in range(seq // blk):
        cols = slice(r * blk, (r + 1) * blk)
        o_ref[0, :, cols] = _norm_rows(hs_scr[:, cols]).astype(o_ref.dtype)


def _scan_call(qt, k, vt, grow):
    bsz, seq, _ = k.shape
    nc, lc = grow.shape[3], grow.shape[4]
    feat = pl.BlockSpec((1, M_DH, seq), lambda b, h: (b, h, 0))
    return pl.pallas_call(
        _scan_kernel,
        out_shape=jax.ShapeDtypeStruct((bsz, D_MODEL, seq), BF16),
        grid=(bsz, M_HEADS),
        in_specs=[feat, pl.BlockSpec((1, seq, M_DH), lambda b, h: (b, 0, h)), feat,
                  pl.BlockSpec((1, 1, 4, nc, lc), lambda b, h: (b, h, 0, 0, 0))],
        out_specs=feat,
        scratch_shapes=[pltpu.VMEM((M_DH, seq), F32), pltpu.VMEM((2, M_DH + AUG, M_DH), F32)],
        compiler_params=_params(2),
        name="mlstm_scan",
    )(qt, k, vt, grow)


def _mix_kernel(x_ref, hnt_ref, kp_ref, k_ref, kn_ref, vtp_ref, vt_ref, vtn_ref, sh_ref, sc_ref, gm_ref, sink_ref,
                wt_ref, bt_ref, nw_ref, clw_ref, clb_ref, ws_ref, bs_ref, pmt_ref, pat_ref, pct_ref, wo_ref,
                l1w_ref, l1b_ref, o_ref, kx_scr, vtx_scr, yat_scr, yct_scr):
    t = x_ref.shape[1]
    i = pl.program_id(1)
    last = pl.num_programs(1) - 1
    nqb = t // WINDOW
    x = x_ref[0]
    h = (x * (1.0 + sc_ref[0]) + sh_ref[0]).astype(BF16)

    def proj_t(off, n):
        return _dot_nt(wt_ref[off:off + n, :], h) + _lanes(bt_ref[off:off + n, :], t)

    y_mt = (hnt_ref[0].astype(F32) * _lanes(nw_ref[...], t) * _sigmoid(proj_t(MIX_ZM, D_MODEL))).astype(BF16)
    acc = _sigmoid(proj_t(MIX_G, D_MODEL)) * _dot(pmt_ref[...], y_mt)

    kx_scr[0:WINDOW, :] = kp_ref[0]
    kx_scr[WINDOW:WINDOW + t, :] = k_ref[0]
    kx_scr[WINDOW + t:, :] = kn_ref[0]
    vtx_scr[:, 0:WINDOW] = vtp_ref[0]
    vtx_scr[:, WINDOW:WINDOW + t] = vt_ref[0]
    vtx_scr[:, WINDOW + t:] = vtn_ref[0]
    qt = (proj_t(MIX_Q, D_A) * (A_DH ** -0.5)).astype(BF16)
    kj = lax.broadcasted_iota(jnp.int32, (3 * WINDOW, WINDOW), 0)
    qi = lax.broadcasted_iota(jnp.int32, (3 * WINDOW, WINDOW), 1)
    dist = jnp.abs(qi + WINDOW - kj)
    band = dist <= WINDOW
    distf = dist.astype(F32)
    first_pen = jnp.where(i == 0, -jnp.inf, 0.0)
    last_pen = jnp.where(i == last, -jnp.inf, 0.0)
    ones_rows = jnp.ones((AUG, 3 * WINDOW), BF16)
    for g in range(A_KV_HEADS):
        bias = jnp.concatenate(
            [jnp.where(band, -(2.0 ** (-8.0 * (g * A_REP + r + 1) / A_HEADS)) * distf, -jnp.inf)
             for r in range(A_REP)], axis=1)
        sink = jnp.concatenate([jnp.full((1, WINDOW), sink_ref[g * A_REP + r], F32) for r in range(A_REP)], axis=1)
        for n in range(nqb):
            qb = jnp.concatenate([qt[(g * A_REP + r) * A_DH:(g * A_REP + r + 1) * A_DH, n * WINDOW:(n + 1) * WINDOW]
                                  for r in range(A_REP)], axis=1)
            kb = kx_scr[n * WINDOW:(n + 3) * WINDOW, g * A_DH:(g + 1) * A_DH]
            vtb = jnp.concatenate([vtx_scr[g * A_DH:(g + 1) * A_DH, n * WINDOW:(n + 3) * WINDOW], ones_rows], axis=0)
            blk_bias = bias
            if n == 0:
                blk_bias = jnp.concatenate([bias[:WINDOW] + first_pen, bias[WINDOW:]], axis=0)
            if n == nqb - 1:
                blk_bias = jnp.concatenate([blk_bias[:2 * WINDOW], blk_bias[2 * WINDOW:] + last_pen], axis=0)
            s = _dot(kb, qb) + blk_bias
            mx = jnp.maximum(jnp.max(s, axis=0, keepdims=True), sink)
            p = jnp.exp(s - mx).astype(BF16)
            ot = _dot(vtb, p)
            ot = ot[:A_DH] * (1.0 / (ot[A_DH:A_DH + 1] + jnp.exp(sink - mx)))
            for r in range(A_REP):
                r0 = (g * A_REP + r) * A_DH
                yat_scr[r0:r0 + A_DH, n * WINDOW:(n + 1) * WINDOW] = ot[:, r * WINDOW:(r + 1) * WINDOW]
    acc = acc + _sigmoid(proj_t(MIX_G + D_MODEL, D_MODEL)) * _dot(pat_ref[...], yat_scr[...].astype(BF16))

    uvt = _gelu_tanh(proj_t(MIX_UV, 2 * D_C))
    vnt = (_norm_rows(uvt[D_C:]) * _lanes(clw_ref[...], t) + _lanes(clb_ref[...], t)).astype(BF16)
    gw = D_C // C_GROUPS
    for n in range(t // C_CHUNK):
        for g in range(C_GROUPS):
            vst = _dot_nt(vnt[g * gw:(g + 1) * gw, n * C_CHUNK:(n + 1) * C_CHUNK], ws_ref[g]) + bs_ref[g:g + 1, :]
            yct_scr[g * gw:(g + 1) * gw, n * C_CHUNK:(n + 1) * C_CHUNK] = vst
    y_ct = (uvt[:D_C] * yct_scr[...]).astype(BF16)
    acc = acc + _sigmoid(proj_t(MIX_G + 2 * D_MODEL, D_MODEL)) * _dot(pct_ref[...], y_ct)

    mix = _dot_tn(acc.astype(BF16), wo_ref[...])
    o_ref[0] = _layer_norm(ALPHA * x + (1.0 + gm_ref[0]) * mix, l1w_ref[...], l1b_ref[...])


def _mix_call(x, hnt, ak, avt, sh, sc, gm, w):
    bsz, seq, _ = x.shape
    t = min(TOKEN_TILE, seq)
    nt = seq // t
    wb = t // WINDOW
    nwb = seq // WINDOW
    tok = lambda n: pl.BlockSpec((1, t, n), lambda b, i: (b, i, 0))
    feat = lambda n: pl.BlockSpec((1, n, t), lambda b, i: (b, 0, i))
    vec = pl.BlockSpec((1, 1, D_MODEL), lambda b, i: (b, 0, 0))
    prev_blk = lambda b, i: jnp.maximum(i * wb - 1, 0)
    next_blk = lambda b, i: jnp.minimum((i + 1) * wb, nwb - 1)
    return pl.pallas_call(
        _mix_kernel,
        out_shape=jax.ShapeDtypeStruct((bsz, seq, D_MODEL), F32),
        grid=(bsz, nt),
        in_specs=[tok(D_MODEL), feat(D_MODEL),
                  pl.BlockSpec((1, WINDOW, A_KV), lambda b, i: (b, prev_blk(b, i), 0)),
                  tok(A_KV),
                  pl.BlockSpec((1, WINDOW, A_KV), lambda b, i: (b, next_blk(b, i), 0)),
                  pl.BlockSpec((1, A_KV, WINDOW), lambda b, i: (b, 0, prev_blk(b, i))),
                  feat(A_KV),
                  pl.BlockSpec((1, A_KV, WINDOW), lambda b, i: (b, 0, next_blk(b, i))),
                  vec, vec, vec,
                  pl.BlockSpec(memory_space=pltpu.SMEM),
                  _const_spec((N_MIX, D_MODEL)), _const_spec((N_MIX, V7X_LANES)),
                  _const_spec((D_MODEL, V7X_LANES)), _const_spec((D_C, V7X_LANES)), _const_spec((D_C, V7X_LANES)),
                  _const_spec((C_GROUPS, C_CHUNK, C_CHUNK)), _const_spec((C_GROUPS, C_CHUNK)),
                  _const_spec((D_MODEL, D_MODEL)), _const_spec((D_MODEL, D_A)), _const_spec((D_MODEL, D_C)),
                  _const_spec((D_MODEL, D_MODEL)), _const_spec((1, D_MODEL)), _const_spec((1, D_MODEL))],
        out_specs=tok(D_MODEL),
        scratch_shapes=[pltpu.VMEM((t + 2 * WINDOW, A_KV), BF16), pltpu.VMEM((A_KV, t + 2 * WINDOW), BF16),
                        pltpu.VMEM((D_A, t), F32), pltpu.VMEM((D_C, t), F32)],
        compiler_params=_params(2),
        name="mix_merge_ln1",
    )(x, hnt, ak, ak, ak, avt, avt, avt, sh, sc, gm, w["sink"], w["w_mixt"], w["b_mixt"], w["norm_w"],
      w["c_ln_w"], w["c_ln_b"], w["c_ws"], w["c_bs"], w["p_mt"], w["p_at"], w["p_ct"], w["w_out"],
      w["ln1_w"], w["ln1_b"])


def _mlp_kernel(x_ref, sh_ref, sc_ref, gm_ref, w1_ref, b1_ref, w2_ref, b2_ref, lw_ref, lb_ref, o_ref):
    x = x_ref[0]
    h = (x * (1.0 + sc_ref[0]) + sh_ref[0]).astype(BF16)
    ff = b2_ref[...]
    for c in range(D_FF // FF_CHUNK):
        cs = slice(c * FF_CHUNK, (c + 1) * FF_CHUNK)
        hid = jnp.maximum(_dot(h, w1_ref[:, cs]) + b1_ref[:, cs], 0.0)
        ff = ff + _dot((hid * hid).astype(BF16), w2_ref[cs, :])
    o_ref[0] = _layer_norm(ALPHA * x + (1.0 + gm_ref[0]) * ff, lw_ref[...], lb_ref[...])


def _mlp_call(x, sh, sc, gm, w):
    bsz, seq, _ = x.shape
    t = min(TOKEN_TILE, seq)
    tile = pl.BlockSpec((1, t, D_MODEL), lambda b, i: (b, i, 0))
    vec = pl.BlockSpec((1, 1, D_MODEL), lambda b, i: (b, 0, 0))
    return pl.pallas_call(
        _mlp_kernel,
        out_shape=jax.ShapeDtypeStruct((bsz, seq, D_MODEL), F32),
        grid=(bsz, seq // t),
        in_specs=[tile, vec, vec, vec,
                  _const_spec((D_MODEL, D_FF)), _const_spec((1, D_FF)),
                  _const_spec((D_FF, D_MODEL)), _const_spec((1, D_MODEL)),
                  _const_spec((1, D_MODEL)), _const_spec((1, D_MODEL))],
        out_specs=tile,
        compiler_params=_params(2),
        name="mlp_ln2",
    )(x, sh, sc, gm, w["w1"], w["b1"], w["w2"], w["b2"], w["ln2_w"], w["ln2_b"])


def _layer_weights(p, l):
    w_in, b_in = p["w_in"][l], p["b_in"][l]
    gate_pad = GATE_PAD - 4 * M_HEADS
    row = lambda name: p[name][l][None, :]
    col = lambda v: jnp.broadcast_to(v[:, None], (v.shape[0], V7X_LANES))
    w_pre = jnp.concatenate([w_in[:, OFF_MX:OFF_MZ], w_in[:, OFF_AK:OFF_AV], w_in[:, OFF_MG:OFF_AQ],
                             jnp.zeros((D_MODEL, gate_pad), F32)], axis=1)
    b_pre = jnp.concatenate([b_in[OFF_MX:OFF_MZ], b_in[OFF_AK:OFF_AV], b_in[OFF_MG:OFF_AQ],
                             jnp.zeros((gate_pad,), F32)])
    w_mix = jnp.concatenate([w_in[:, OFF_MZ:OFF_MG], w_in[:, OFF_AQ:OFF_AK], w_in[:, OFF_C:]], axis=1)
    b_mix = jnp.concatenate([b_in[OFF_MZ:OFF_MG], b_in[OFF_AQ:OFF_AK], b_in[OFF_C:]])
    return dict(
        w_pre=w_pre.astype(BF16), b_pre=b_pre[None, :],
        w_avt=w_in[:, OFF_AV:OFF_C].T.astype(BF16), b_avt=col(b_in[OFF_AV:OFF_C]),
        w_mixt=w_mix.T.astype(BF16), b_mixt=col(b_mix),
        conv_w=p["m_conv_w"][l], conv_b=row("m_conv_b"),
        wqt=jnp.transpose(p["m_wq"][l], (0, 2, 1)).astype(BF16), wk=p["m_wk"][l].astype(BF16),
        wvt=jnp.transpose(p["m_wv"][l], (0, 2, 1)).astype(BF16),
        norm_w=col(p["m_norm_w"][l]), sink=p["a_sink"][l],
        c_ln_w=col(p["c_ln_w"][l]), c_ln_b=col(p["c_ln_b"][l]),
        c_ws=p["c_ws"][l].astype(BF16), c_bs=p["c_bs"][l],
        p_mt=p["p_m"][l].T.astype(BF16), p_at=p["p_a"][l].T.astype(BF16), p_ct=p["p_c"][l].T.astype(BF16),
        w_out=p["w_out"][l].astype(BF16), ln1_w=row("ln1_w"), ln1_b=row("ln1_b"),
        w1=p["mlp_w1"][l].astype(BF16), b1=row("mlp_b1"), w2=p["mlp_w2"][l].astype(BF16), b2=row("mlp_b2"),
        ln2_w=row("ln2_w"), ln2_b=row("ln2_b"))


def _trunk(x, c, ada_w, ada_b, weights):
    bsz, seq, _ = x.shape
    lc = min(SCAN_CHUNK, seq)
    nc = seq // lc
    mod = _modulation(c, ada_w, ada_b)
    for l in range(DEPTH):
        w = weights[l]
        sh1, sc1, g1, sh2, sc2, g2 = [mod[l, :, None, k * D_MODEL:(k + 1) * D_MODEL] for k in range(6)]
        qt, k, vt, gates, ak, avt = _pre_call(x, sh1, sc1, w)
        grow = jnp.transpose(gates.reshape(bsz, nc, lc, 4, M_HEADS), (0, 4, 3, 1, 2))
        hnt = _scan_call(qt, k, vt, grow)
        x = _mix_call(x, hnt, ak, avt, sh1, sc1, g1, w)
        x = _mlp_call(x, sh2, sc2, g2, w)
    return x


def kernel(x_prompt, x_sample, c_prompt, c_sample, ada_w, ada_b, w_in, b_in, m_conv_w, m_conv_b, m_wq, m_wk, m_wv,
           m_norm_w, a_sink, c_ln_w, c_ln_b, c_ws, c_bs, p_m, p_a, p_c, w_out, ln1_w, ln1_b, mlp_w1, mlp_b1,
           mlp_w2, mlp_b2, ln2_w, ln2_b):
    p = dict(w_in=w_in, b_in=b_in, m_conv_w=m_conv_w, m_conv_b=m_conv_b, m_wq=m_wq,
             m_wk=m_wk, m_wv=m_wv, m_norm_w=m_norm_w, a_sink=a_sink, c_ln_w=c_ln_w, c_ln_b=c_ln_b, c_ws=c_ws,
             c_bs=c_bs, p_m=p_m, p_a=p_a, p_c=p_c, w_out=w_out, ln1_w=ln1_w, ln1_b=ln1_b, mlp_w1=mlp_w1,
             mlp_b1=mlp_b1, mlp_w2=mlp_w2, mlp_b2=mlp_b2, ln2_w=ln2_w, ln2_b=ln2_b)
    weights = [_layer_weights(p, l) for l in range(DEPTH)]
    ada_wb = ada_w.astype(BF16)
    ada_b3 = ada_b.reshape(DEPTH, 1, -1)
    return (_trunk(x_prompt, c_prompt, ada_wb, ada_b3, weights), _trunk(x_sample, c_sample, ada_wb, ada_b3, weights))
```

```python
import jax
import jax.numpy as jnp
from jax import lax
from jax.experimental import pallas as pl
from jax.experimental.pallas import tpu as pltpu

F32 = jnp.float32
BF16 = jnp.bfloat16

D_MODEL = 1024
M_HEADS = 4
M_DH = 256
CONV_K = 5
A_HEADS = 8
A_KV_HEADS = 2
A_DH = 64
A_REP = A_HEADS // A_KV_HEADS
D_A = A_HEADS * A_DH
A_KV = A_KV_HEADS * A_DH
WINDOW = 128
D_C = 512
C_GROUPS = 4
C_CHUNK = 128
D_FF = 4096
N_BRANCH = 3
DEPTH = 2
ALPHA = (2 * DEPTH) ** 0.25
LN_EPS = 1e-5

OFF_MX = 0
OFF_MZ = OFF_MX + D_MODEL
OFF_MG = OFF_MZ + D_MODEL
OFF_AQ = OFF_MG + 4 * M_HEADS
OFF_AK = OFF_AQ + D_A
OFF_AV = OFF_AK + A_KV
OFF_C = OFF_AV + A_KV
OFF_G = OFF_C + 2 * D_C

V7X_LANES = 128
V7X_SUBLANES_F32 = 8
V7X_SUBLANES_BF16 = 16
V7X_VMEM_LIMIT_BYTES = 56 * 1024 * 1024

SCAN_CHUNK = 256
TOKEN_TILE = 512
HALO = V7X_SUBLANES_F32
FF_CHUNK = 1024
GATE_PAD = V7X_LANES
SCAN_HEADS = 2
LOG2E = 1.4426950408889634
AUG = V7X_SUBLANES_BF16
N_PRE = D_MODEL + A_KV + GATE_PAD
N_MIX = D_MODEL + D_A + 2 * D_C + N_BRANCH * D_MODEL
MIX_ZM = 0
MIX_Q = MIX_ZM + D_MODEL
MIX_UV = MIX_Q + D_A
MIX_G = MIX_UV + 2 * D_C


def _dot(a, b):
    return jnp.dot(a, b, preferred_element_type=F32)


def _dot_nt(a, b):
    return lax.dot_general(a, b, (((1,), (1,)), ((), ())), preferred_element_type=F32)


def _dot_tn(a, b):
    return lax.dot_general(a, b, (((0,), (0,)), ((), ())), preferred_element_type=F32)


def _sigmoid(x):
    return 0.5 * (1.0 + jnp.tanh(0.5 * x))


def _log_sigmoid(x):
    return jnp.minimum(x, 0.0) - jnp.log(1.0 + jnp.exp(-jnp.abs(x)))


def _gelu_tanh(x):
    return 0.5 * x * (1.0 + jnp.tanh(0.7978845608028654 * (x + 0.044715 * (x * x * x))))


def _layer_norm(x, w, b):
    mu = jnp.mean(x, axis=-1, keepdims=True)
    xc = x - mu
    var = jnp.mean(xc * xc, axis=-1, keepdims=True)
    return xc * lax.rsqrt(var + LN_EPS) * w + b


def _norm_rows(x):
    mu = jnp.mean(x, axis=0, keepdims=True)
    xc = x - mu
    var = jnp.mean(xc * xc, axis=0, keepdims=True)
    return xc * lax.rsqrt(var + LN_EPS)


def _lanes(col, n):
    return jnp.concatenate([col] * (n // V7X_LANES), axis=1)


def _const_spec(shape):
    nd = len(shape)
    return pl.BlockSpec(shape, lambda *_: (0,) * nd, pipeline_mode=pl.Buffered(1))


def _params(n_axes):
    return pltpu.CompilerParams(dimension_semantics=("arbitrary",) * n_axes,
                                vmem_limit_bytes=V7X_VMEM_LIMIT_BYTES)


def _mod_kernel(c_ref, w_ref, b_ref, o_ref):
    c = c_ref[...]
    s = (c * _sigmoid(c)).astype(BF16)
    o_ref[0] = _dot(s, w_ref[0]) + b_ref[0]


def _modulation(c, ada_w, ada_b):
    n_layers, _, n_out = ada_w.shape
    bsz = c.shape[0]
    tn = 1536
    return pl.pallas_call(
        _mod_kernel,
        out_shape=jax.ShapeDtypeStruct((n_layers, bsz, n_out), F32),
        grid=(n_layers, n_out // tn),
        in_specs=[pl.BlockSpec((bsz, D_MODEL), lambda l, j: (0, 0)),
                  pl.BlockSpec((1, D_MODEL, tn), lambda l, j: (l, 0, j)),
                  pl.BlockSpec((1, 1, tn), lambda l, j: (l, 0, j))],
        out_specs=pl.BlockSpec((1, bsz, tn), lambda l, j: (l, 0, j)),
        compiler_params=_params(2),
        name="adaln_mod",
    )(c, ada_w, ada_b)


def _pre_kernel(xp_ref, x_ref, xn_ref, sh_ref, sc_ref, w_ref, b_ref, wav_ref, bav_ref, cw_ref, cb_ref,
                wqt_ref, wk_ref, wvt_ref, qt_ref, k_ref, vt_ref, g_ref, ak_ref, avt_ref, xs_scr, xm_scr):
    t = x_ref.shape[1]
    i = pl.program_id(1)
    last = pl.num_programs(1) - 1
    scale = 1.0 + sc_ref[0]
    shift = sh_ref[0]
    xs_scr[0:HALO, :] = xp_ref[0]
    xs_scr[HALO:HALO + t, :] = x_ref[0]
    xs_scr[HALO + t:, :] = xn_ref[0]
    h_ext = (xs_scr[...] * scale + shift).astype(BF16)
    proj = _dot(h_ext, w_ref[...]) + b_ref[...]
    row = lax.broadcasted_iota(jnp.int32, (t + 2 * HALO, 1), 0)
    lo = jnp.where(i > 0, 0, HALO)
    hi = jnp.where(i < last, t + 2 * HALO, t + HALO)
    inside = jnp.logical_and(row >= lo, row < hi)
    xm_scr[...] = jnp.where(inside, proj[:, :D_MODEL], 0.0)
    main = proj[HALO:HALO + t]
    ak_ref[0] = main[:, D_MODEL:D_MODEL + A_KV].astype(BF16)
    g_ref[0] = main[:, D_MODEL + A_KV:D_MODEL + A_KV + 4 * M_HEADS]
    h = (x_ref[0] * scale + shift).astype(BF16)
    avt_ref[0] = (_dot_nt(wav_ref[...], h) + _lanes(bav_ref[...], t)).astype(BF16)
    conv = cb_ref[...]
    for j in range(CONV_K):
        off = HALO - CONV_K // 2 + j
        conv = conv + cw_ref[j:j + 1, :] * xm_scr[off:off + t, :]
    xc = (conv * _sigmoid(conv)).astype(BF16)
    xm = xm_scr[HALO:HALO + t, :].astype(BF16)
    for hd in range(M_HEADS):
        sl = slice(hd * M_DH, (hd + 1) * M_DH)
        qt_ref[0, sl, :] = _dot_nt(wqt_ref[hd], xc[:, sl]).astype(BF16)
        k_ref[0, :, sl] = (_dot(xc[:, sl], wk_ref[hd]) * (M_DH ** -0.5)).astype(BF16)
        vt_ref[0, sl, :] = _dot_nt(wvt_ref[hd], xm[:, sl]).astype(BF16)


def _pre_call(x, sh, sc, w):
    bsz, seq, _ = x.shape
    t = min(TOKEN_TILE, seq)
    nt = seq // t
    hb = t // HALO
    nhb = seq // HALO
    tok = lambda n: pl.BlockSpec((1, t, n), lambda b, i: (b, i, 0))
    feat = lambda n: pl.BlockSpec((1, n, t), lambda b, i: (b, 0, i))
    vec = pl.BlockSpec((1, 1, D_MODEL), lambda b, i: (b, 0, 0))
    tok_shape = lambda n, dt: jax.ShapeDtypeStruct((bsz, seq, n), dt)
    feat_shape = lambda n: jax.ShapeDtypeStruct((bsz, n, seq), BF16)
    return pl.pallas_call(
        _pre_kernel,
        out_shape=(feat_shape(D_MODEL), tok_shape(D_MODEL, BF16), feat_shape(D_MODEL),
                   tok_shape(4 * M_HEADS, F32), tok_shape(A_KV, BF16), feat_shape(A_KV)),
        grid=(bsz, nt),
        in_specs=[pl.BlockSpec((1, HALO, D_MODEL), lambda b, i: (b, jnp.maximum(i * hb - 1, 0), 0)),
                  tok(D_MODEL),
                  pl.BlockSpec((1, HALO, D_MODEL), lambda b, i: (b, jnp.minimum((i + 1) * hb, nhb - 1), 0)),
                  vec, vec,
                  _const_spec((D_MODEL, N_PRE)), _const_spec((1, N_PRE)),
                  _const_spec((A_KV, D_MODEL)), _const_spec((A_KV, V7X_LANES)),
                  _const_spec((CONV_K, D_MODEL)), _const_spec((1, D_MODEL)),
                  _const_spec((M_HEADS, M_DH, M_DH)), _const_spec((M_HEADS, M_DH, M_DH)),
                  _const_spec((M_HEADS, M_DH, M_DH))],
        out_specs=(feat(D_MODEL), tok(D_MODEL), feat(D_MODEL), tok(4 * M_HEADS), tok(A_KV), feat(A_KV)),
        scratch_shapes=[pltpu.VMEM((t + 2 * HALO, D_MODEL), F32), pltpu.VMEM((t + 2 * HALO, D_MODEL), F32)],
        compiler_params=_params(2),
        name="pre_proj_conv_qkv",
    )(x, x, x, sh, sc, w["w_pre"], w["b_pre"], w["w_avt"], w["b_avt"], w["conv_w"], w["conv_b"],
      w["wqt"], w["wk"], w["wvt"])


def _scan_kernel(qt_ref, k_ref, vt_ref, gr_ref, o_ref, hs_scr, c_scr, sc_scr):
    seq = k_ref.shape[1]
    lc = gr_ref.shape[4]
    nc = seq // lc
    nh = gr_ref.shape[1]
    si = lax.broadcasted_iota(jnp.int32, (lc, lc), 0)
    li = lax.broadcasted_iota(jnp.int32, (lc, lc), 1)
    masks = (si <= li, si >= li)
    ones_rows = jnp.ones((AUG, lc), BF16)
    c_scr[...] = jnp.zeros_like(c_scr)

    gate = {}
    for hd in range(nh):
        for d in range(2):
            ig = gr_ref[0, hd, 2 * d] * LOG2E
            lf = _log_sigmoid(gr_ref[0, hd, 2 * d + 1]) * LOG2E
            a = jnp.dot(lf, masks[d].astype(F32), precision=lax.Precision.HIGHEST, preferred_element_type=F32)
            w = ig - a
            a_tot = a[:, lc - 1:lc] if d == 0 else a[:, 0:1]
            w_max = jnp.max(w, axis=1, keepdims=True)
            w_cols = jnp.concatenate([w, jnp.zeros((V7X_LANES - nc, lc), F32)], axis=0).T
            m = jnp.zeros((1, 1), F32)
            m_seq = []
            for step in range(nc):
                jc = step if d == 0 else nc - 1 - step
                m_next = a_tot[jc:jc + 1] + jnp.maximum(m, w_max[jc:jc + 1])
                m_seq.append((m, m_next))
                m = m_next
            gate[hd, d] = (a, w, a_tot, w_cols, m_seq)

    for hd in range(nh):
        rows = slice(hd * M_DH, (hd + 1) * M_DH)
        for jc in range(nc):
            cols = slice(jc * lc, (jc + 1) * lc)
            sc_scr[hd, jc] = _dot(k_ref[0, cols, rows], qt_ref[0, rows, cols])

    for step in range(nc):
        for hd in range(nh):
            rows = slice(hd * M_DH, (hd + 1) * M_DH)
            for d in range(2):
                a, w, a_tot, w_cols, m_seq = gate[hd, d]
                jc = step if d == 0 else nc - 1 - step
                cols = slice(jc * lc, (jc + 1) * lc)
                m, m_next = m_seq[step]
                qt = qt_ref[0, rows, cols]
                k = k_ref[0, cols, rows]
                vta = jnp.concatenate([vt_ref[0, rows, cols], ones_rows], axis=0)
                wm = jnp.where(masks[d], w_cols[:, jc:jc + 1], -jnp.inf)
                u = -jnp.maximum(m, jnp.max(wm, axis=0, keepdims=True))
                st = sc_scr[hd, jc] * jnp.exp2(wm + u)
                big = (_dot(vta, st.astype(BF16))
                       + _dot(c_scr[hd, d].astype(BF16), qt * jnp.exp2(m + u).astype(BF16)))
                den = jnp.maximum(jnp.abs(big[M_DH:M_DH + 1, :]), jnp.exp2(u - a[jc:jc + 1, :]))
                ht = big[:M_DH, :] * (1.0 / den)
                if step < nc // 2:
                    hs_scr[rows, cols] = ht
                else:
                    hs_scr[rows, cols] = hs_scr[rows, cols] + ht
                wg = jnp.exp2(a_tot[jc:jc + 1] + w[jc:jc + 1, :] - m_next)
                decay = jnp.exp2(a_tot[jc:jc + 1] + m - m_next)
                c_scr[hd, d] = decay * c_scr[hd, d] + _dot(vta * wg.astype(BF16), k)

    blk = min(512, seq)
    for hd in range(nh):
        rows = slice(hd * M_DH, (hd + 1) * M_DH)
        for r in range(seq // blk):
            cols = slice(r * blk, (r + 1) * blk)
            o_ref[0, rows, cols] = _norm_rows(hs_scr[rows, cols]).astype(o_ref.dtype)


def _scan_call(qt, k, vt, grow):
    bsz, seq, _ = k.shape
    nc, lc = grow.shape[3], grow.shape[4]
    nh = SCAN_HEADS
    feat = pl.BlockSpec((1, nh * M_DH, seq), lambda b, h: (b, h, 0))
    return pl.pallas_call(
        _scan_kernel,
        out_shape=jax.ShapeDtypeStruct((bsz, D_MODEL, seq), BF16),
        grid=(bsz, M_HEADS // nh),
        in_specs=[feat, pl.BlockSpec((1, seq, nh * M_DH), lambda b, h: (b, 0, h)), feat,
                  pl.BlockSpec((1, nh, 4, nc, lc), lambda b, h: (b, h, 0, 0, 0))],
        out_specs=feat,
        scratch_shapes=[pltpu.VMEM((nh * M_DH, seq), F32), pltpu.VMEM((nh, 2, M_DH + AUG, M_DH), F32),
                        pltpu.VMEM((nh, nc, lc, lc), F32)],
        compiler_params=_params(2),
        name="mlstm_scan",
    )(qt, k, vt, grow)


def _mix_kernel(x_ref, hnt_ref, kp_ref, k_ref, kn_ref, vtp_ref, vt_ref, vtn_ref, sh_ref, sc_ref, gm_ref, sink_ref,
                wt_ref, bt_ref, nw_ref, clw_ref, clb_ref, ws_ref, bs_ref, pmt_ref, pat_ref, pct_ref, wo_ref,
                l1w_ref, l1b_ref, o_ref, kx_scr, vtx_scr, yat_scr, yct_scr, s_scr):
    t = x_ref.shape[1]
    i = pl.program_id(1)
    last = pl.num_programs(1) - 1
    nqb = t // WINDOW
    x = x_ref[0]
    h = (x * (1.0 + sc_ref[0]) + sh_ref[0]).astype(BF16)

    def proj_t(off, n):
        return _dot_nt(wt_ref[off:off + n, :], h) + _lanes(bt_ref[off:off + n, :], t)

    p_zm = proj_t(MIX_ZM, D_MODEL)
    p_q = proj_t(MIX_Q, D_A)
    p_uv = proj_t(MIX_UV, 2 * D_C)
    p_g0 = proj_t(MIX_G, D_MODEL)
    p_g1 = proj_t(MIX_G + D_MODEL, D_MODEL)
    p_g2 = proj_t(MIX_G + 2 * D_MODEL, D_MODEL)

    y_mt = (hnt_ref[0].astype(F32) * _lanes(nw_ref[...], t) * _sigmoid(p_zm)).astype(BF16)
    acc = _sigmoid(p_g0) * _dot(pmt_ref[...], y_mt)

    kx_scr[0:WINDOW, :] = kp_ref[0]
    kx_scr[WINDOW:WINDOW + t, :] = k_ref[0]
    kx_scr[WINDOW + t:, :] = kn_ref[0]
    vtx_scr[:, 0:WINDOW] = vtp_ref[0]
    vtx_scr[:, WINDOW:WINDOW + t] = vt_ref[0]
    vtx_scr[:, WINDOW + t:] = vtn_ref[0]
    qt = (p_q * (LOG2E * A_DH ** -0.5)).astype(BF16)
    kj = lax.broadcasted_iota(jnp.int32, (3 * WINDOW, WINDOW), 0)
    qi = lax.broadcasted_iota(jnp.int32, (3 * WINDOW, WINDOW), 1)
    dist = jnp.abs(qi + WINDOW - kj)
    band = dist <= WINDOW
    distf = dist.astype(F32)
    first_pen = jnp.where(i == 0, -jnp.inf, 0.0)
    last_pen = jnp.where(i == last, -jnp.inf, 0.0)
    ones_rows = jnp.ones((AUG, 3 * WINDOW), BF16)
    for g in range(A_KV_HEADS):
        for n in range(nqb):
            qb = jnp.concatenate([qt[(g * A_REP + r) * A_DH:(g * A_REP + r + 1) * A_DH, n * WINDOW:(n + 1) * WINDOW]
                                  for r in range(A_REP)], axis=1)
            kb = kx_scr[n * WINDOW:(n + 3) * WINDOW, g * A_DH:(g + 1) * A_DH]
            s_scr[g * nqb + n] = _dot(kb, qb)
    for g in range(A_KV_HEADS):
        bias = jnp.concatenate(
            [jnp.where(band, -(LOG2E * 2.0 ** (-8.0 * (g * A_REP + r + 1) / A_HEADS)) * distf, -jnp.inf)
             for r in range(A_REP)], axis=1)
        sink = jnp.concatenate([jnp.full((1, WINDOW), sink_ref[g * A_REP + r] * LOG2E, F32) for r in range(A_REP)],
                               axis=1)
        for n in range(nqb):
            vtb = jnp.concatenate([vtx_scr[g * A_DH:(g + 1) * A_DH, n * WINDOW:(n + 3) * WINDOW], ones_rows], axis=0)
            blk_bias = bias
            if n == 0:
                blk_bias = jnp.concatenate([bias[:WINDOW] + first_pen, bias[WINDOW:]], axis=0)
            if n == nqb - 1:
                blk_bias = jnp.concatenate([blk_bias[:2 * WINDOW], blk_bias[2 * WINDOW:] + last_pen], axis=0)
            s = s_scr[g * nqb + n] + blk_bias
            mx = jnp.maximum(jnp.max(s, axis=0, keepdims=True), sink)
            p = jnp.exp2(s - mx).astype(BF16)
            ot = _dot(vtb, p)
            ot = ot[:A_DH] * (1.0 / (ot[A_DH:A_DH + 1] + jnp.exp2(sink - mx)))
            for r in range(A_REP):
                r0 = (g * A_REP + r) * A_DH
                yat_scr[r0:r0 + A_DH, n * WINDOW:(n + 1) * WINDOW] = ot[:, r * WINDOW:(r + 1) * WINDOW]
    acc = acc + _sigmoid(p_g1) * _dot(pat_ref[...], yat_scr[...].astype(BF16))

    uvt = _gelu_tanh(p_uv)
    vnt = (_norm_rows(uvt[D_C:]) * _lanes(clw_ref[...], t) + _lanes(clb_ref[...], t)).astype(BF16)
    gw = D_C // C_GROUPS
    for n in range(t // C_CHUNK):
        for g in range(C_GROUPS):
            vst = _dot_nt(vnt[g * gw:(g + 1) * gw, n * C_CHUNK:(n + 1) * C_CHUNK], ws_ref[g]) + bs_ref[g:g + 1, :]
            yct_scr[g * gw:(g + 1) * gw, n * C_CHUNK:(n + 1) * C_CHUNK] = vst
    y_ct = (uvt[:D_C] * yct_scr[...]).astype(BF16)
    acc = acc + _sigmoid(p_g2) * _dot(pct_ref[...], y_ct)

    mix = _dot_tn(acc.astype(BF16), wo_ref[...])
    o_ref[0] = _layer_norm(ALPHA * x + (1.0 + gm_ref[0]) * mix, l1w_ref[...], l1b_ref[...])


def _mix_call(x, hnt, ak, avt, sh, sc, gm, w):
    bsz, seq, _ = x.shape
    t = min(TOKEN_TILE, seq)
    nt = seq // t
    wb = t // WINDOW
    nwb = seq // WINDOW
    tok = lambda n: pl.BlockSpec((1, t, n), lambda b, i: (b, i, 0))
    feat = lambda n: pl.BlockSpec((1, n, t), lambda b, i: (b, 0, i))
    vec = pl.BlockSpec((1, 1, D_MODEL), lambda b, i: (b, 0, 0))
    prev_blk = lambda b, i: jnp.maximum(i * wb - 1, 0)
    next_blk = lambda b, i: jnp.minimum((i + 1) * wb, nwb - 1)
    return pl.pallas_call(
        _mix_kernel,
        out_shape=jax.ShapeDtypeStruct((bsz, seq, D_MODEL), F32),
        grid=(bsz, nt),
        in_specs=[tok(D_MODEL), feat(D_MODEL),
                  pl.BlockSpec((1, WINDOW, A_KV), lambda b, i: (b, prev_blk(b, i), 0)),
                  tok(A_KV),
                  pl.BlockSpec((1, WINDOW, A_KV), lambda b, i: (b, next_blk(b, i), 0)),
                  pl.BlockSpec((1, A_KV, WINDOW), lambda b, i: (b, 0, prev_blk(b, i))),
                  feat(A_KV),
                  pl.BlockSpec((1, A_KV, WINDOW), lambda b, i: (b, 0, next_blk(b, i))),
                  vec, vec, vec,
                  pl.BlockSpec(memory_space=pltpu.SMEM),
                  _const_spec((N_MIX, D_MODEL)), _const_spec((N_MIX, V7X_LANES)),
                  _const_spec((D_MODEL, V7X_LANES)), _const_spec((D_C, V7X_LANES)), _const_spec((D_C, V7X_LANES)),
                  _const_spec((C_GROUPS, C_CHUNK, C_CHUNK)), _const_spec((C_GROUPS, C_CHUNK)),
                  _const_spec((D_MODEL, D_MODEL)), _const_spec((D_MODEL, D_A)), _const_spec((D_MODEL, D_C)),
                  _const_spec((D_MODEL, D_MODEL)), _const_spec((1, D_MODEL)), _const_spec((1, D_MODEL))],
        out_specs=tok(D_MODEL),
        scratch_shapes=[pltpu.VMEM((t + 2 * WINDOW, A_KV), BF16), pltpu.VMEM((A_KV, t + 2 * WINDOW), BF16),
                        pltpu.VMEM((D_A, t), F32), pltpu.VMEM((D_C, t), F32),
                        pltpu.VMEM((A_KV_HEADS * (t // WINDOW), 3 * WINDOW, A_REP * WINDOW), F32)],
        compiler_params=_params(2),
        name="mix_merge_ln1",
    )(x, hnt, ak, ak, ak, avt, avt, avt, sh, sc, gm, w["sink"], w["w_mixt"], w["b_mixt"], w["norm_w"],
      w["c_ln_w"], w["c_ln_b"], w["c_ws"], w["c_bs"], w["p_mt"], w["p_at"], w["p_ct"], w["w_out"],
      w["ln1_w"], w["ln1_b"])


def _mlp_kernel(x_ref, sh_ref, sc_ref, gm_ref, w1_ref, b1_ref, w2_ref, b2_ref, lw_ref, lb_ref, o_ref):
    x = x_ref[0]
    h = (x * (1.0 + sc_ref[0]) + sh_ref[0]).astype(BF16)
    ff = b2_ref[...]
    for c in range(D_FF // FF_CHUNK):
        cs = slice(c * FF_CHUNK, (c + 1) * FF_CHUNK)
        hid = jnp.maximum(_dot(h, w1_ref[:, cs]) + b1_ref[:, cs], 0.0)
        ff = ff + _dot((hid * hid).astype(BF16), w2_ref[cs, :])
    o_ref[0] = _layer_norm(ALPHA * x + (1.0 + gm_ref[0]) * ff, lw_ref[...], lb_ref[...])


def _mlp_call(x, sh, sc, gm, w):
    bsz, seq, _ = x.shape
    t = min(TOKEN_TILE, seq)
    tile = pl.BlockSpec((1, t, D_MODEL), lambda b, i: (b, i, 0))
    vec = pl.BlockSpec((1, 1, D_MODEL), lambda b, i: (b, 0, 0))
    return pl.pallas_call(
        _mlp_kernel,
        out_shape=jax.ShapeDtypeStruct((bsz, seq, D_MODEL), F32),
        grid=(bsz, seq // t),
        in_specs=[tile, vec, vec, vec,
                  _const_spec((D_MODEL, D_FF)), _const_spec((1, D_FF)),
                  _const_spec((D_FF, D_MODEL)), _const_spec((1, D_MODEL)),
                  _const_spec((1, D_MODEL)), _const_spec((1, D_MODEL))],
        out_specs=tile,
        compiler_params=_params(2),
        name="mlp_ln2",
    )(x, sh, sc, gm, w["w1"], w["b1"], w["w2"], w["b2"], w["ln2_w"], w["ln2_b"])


def _layer_weights(p, l):
    w_in, b_in = p["w_in"][l], p["b_in"][l]
    gate_pad = GATE_PAD - 4 * M_HEADS
    row = lambda name: p[name][l][None, :]
    col = lambda v: jnp.broadcast_to(v[:, None], (v.shape[0], V7X_LANES))
    w_pre = jnp.concatenate([w_in[:, OFF_MX:OFF_MZ], w_in[:, OFF_AK:OFF_AV], w_in[:, OFF_MG:OFF_AQ],
                             jnp.zeros((D_MODEL, gate_pad), F32)], axis=1)
    b_pre = jnp.concatenate([b_in[OFF_MX:OFF_MZ], b_in[OFF_AK:OFF_AV], b_in[OFF_MG:OFF_AQ],
                             jnp.zeros((gate_pad,), F32)])
    w_mix = jnp.concatenate([w_in[:, OFF_MZ:OFF_MG], w_in[:, OFF_AQ:OFF_AK], w_in[:, OFF_C:]], axis=1)
    b_mix = jnp.concatenate([b_in[OFF_MZ:OFF_MG], b_in[OFF_AQ:OFF_AK], b_in[OFF_C:]])
    return dict(
        w_pre=w_pre.astype(BF16), b_pre=b_pre[None, :],
        w_avt=w_in[:, OFF_AV:OFF_C].T.astype(BF16), b_avt=col(b_in[OFF_AV:OFF_C]),
        w_mixt=w_mix.T.astype(BF16), b_mixt=col(b_mix),
        conv_w=p["m_conv_w"][l], conv_b=row("m_conv_b"),
        wqt=jnp.transpose(p["m_wq"][l], (0, 2, 1)).astype(BF16), wk=p["m_wk"][l].astype(BF16),
        wvt=jnp.transpose(p["m_wv"][l], (0, 2, 1)).astype(BF16),
        norm_w=col(p["m_norm_w"][l]), sink=p["a_sink"][l],
        c_ln_w=col(p["c_ln_w"][l]), c_ln_b=col(p["c_ln_b"][l]),
        c_ws=p["c_ws"][l].astype(BF16), c_bs=p["c_bs"][l],
        p_mt=p["p_m"][l].T.astype(BF16), p_at=p["p_a"][l].T.astype(BF16), p_ct=p["p_c"][l].T.astype(BF16),
        w_out=p["w_out"][l].astype(BF16), ln1_w=row("ln1_w"), ln1_b=row("ln1_b"),
        w1=p["mlp_w1"][l].astype(BF16), b1=row("mlp_b1"), w2=p["mlp_w2"][l].astype(BF16), b2=row("mlp_b2"),
        ln2_w=row("ln2_w"), ln2_b=row("ln2_b"))


def _trunk(x, c, ada_w, ada_b, weights):
    bsz, seq, _ = x.shape
    lc = min(SCAN_CHUNK, seq)
    nc = seq // lc
    mod = _modulation(c, ada_w, ada_b)
    for l in range(DEPTH):
        w = weights[l]
        sh1, sc1, g1, sh2, sc2, g2 = [mod[l, :, None, k * D_MODEL:(k + 1) * D_MODEL] for k in range(6)]
        qt, k, vt, gates, ak, avt = _pre_call(x, sh1, sc1, w)
        grow = jnp.transpose(gates.reshape(bsz, nc, lc, 4, M_HEADS), (0, 4, 3, 1, 2))
        hnt = _scan_call(qt, k, vt, grow)
        x = _mix_call(x, hnt, ak, avt, sh1, sc1, g1, w)
        x = _mlp_call(x, sh2, sc2, g2, w)
    return x


def kernel(x_prompt, x_sample, c_prompt, c_sample, ada_w, ada_b, w_in, b_in, m_conv_w, m_conv_b, m_wq, m_wk, m_wv,
           m_norm_w, a_sink, c_ln_w, c_ln_b, c_ws, c_bs, p_m, p_a, p_c, w_out, ln1_w, ln1_b, mlp_w1, mlp_b1,
           mlp_w2, mlp_b2, ln2_w, ln2_b):
    p = dict(w_in=w_in, b_in=b_in, m_conv_w=m_conv_w, m_conv_b=m_conv_b, m_wq=m_wq,
             m_wk=m_wk, m_wv=m_wv, m_norm_w=m_norm_w, a_sink=a_sink, c_ln_w=c_ln_w, c_ln_b=c_ln_b, c_ws=c_ws,
             c_bs=c_bs, p_m=p_m, p_a=p_a, p_c=p_c, w_out=w_out, ln1_w=ln1_w, ln1_b=ln1_b, mlp_w1=mlp_w1,
             mlp_b1=mlp_b1, mlp_w2=mlp_w2, mlp_b2=mlp_b2, ln2_w=ln2_w, ln2_b=ln2_b)
    weights = [_layer_weights(p, l) for l in range(DEPTH)]
    ada_wb = ada_w.astype(BF16)
    ada_b3 = ada_b.reshape(DEPTH, 1, -1)
    return (_trunk(x_prompt, c_prompt, ada_wb, ada_b3, weights), _trunk(x_sample, c_sample, ada_wb, ada_b3, weights))
```

```python
import jax
import jax.numpy as jnp
from jax import lax
from jax.experimental import pallas as pl
from jax.experimental.pallas import tpu as pltpu

F32 = jnp.float32
BF16 = jnp.bfloat16

D_MODEL = 1024
M_HEADS = 4
M_DH = 256
CONV_K = 5
A_HEADS = 8
A_KV_HEADS = 2
A_DH = 64
A_REP = A_HEADS // A_KV_HEADS
D_A = A_HEADS * A_DH
A_KV = A_KV_HEADS * A_DH
WINDOW = 128
D_C = 512
C_GROUPS = 4
C_CHUNK = 128
D_FF = 4096
N_BRANCH = 3
DEPTH = 2
ALPHA = (2 * DEPTH) ** 0.25
LN_EPS = 1e-5

OFF_MX = 0
OFF_MZ = OFF_MX + D_MODEL
OFF_MG = OFF_MZ + D_MODEL
OFF_AQ = OFF_MG + 4 * M_HEADS
OFF_AK = OFF_AQ + D_A
OFF_AV = OFF_AK + A_KV
OFF_C = OFF_AV + A_KV
OFF_G = OFF_C + 2 * D_C

V7X_LANES = 128
V7X_SUBLANES_F32 = 8
V7X_SUBLANES_BF16 = 16
V7X_VMEM_LIMIT_BYTES = 56 * 1024 * 1024

SCAN_CHUNK = 256
TOKEN_TILE = 512
WIDE_TILE = 1024
HALO = V7X_SUBLANES_F32
FF_CHUNK = 1024
N_MOD = 6
SCAN_HEADS = 2
LOG2E = 1.4426950408889634
AUG = V7X_SUBLANES_BF16
N_PRE = D_MODEL + A_KV
N_FT = A_KV + 4 * M_HEADS
N_MIX = D_MODEL + D_A + 2 * D_C + N_BRANCH * D_MODEL
MIX_ZM = 0
MIX_Q = MIX_ZM + D_MODEL
MIX_UV = MIX_Q + D_A
MIX_G = MIX_UV + 2 * D_C


def _dot(a, b):
    return jnp.dot(a, b, preferred_element_type=F32)


def _dot_nt(a, b):
    return lax.dot_general(a, b, (((1,), (1,)), ((), ())), preferred_element_type=F32)


def _dot_tn(a, b):
    return lax.dot_general(a, b, (((0,), (0,)), ((), ())), preferred_element_type=F32)


def _sigmoid(x):
    return 0.5 * (1.0 + jnp.tanh(0.5 * x))


def _log_sigmoid(x):
    return jnp.minimum(x, 0.0) - jnp.log(1.0 + jnp.exp(-jnp.abs(x)))


def _gelu_tanh(x):
    return 0.5 * x * (1.0 + jnp.tanh(0.7978845608028654 * (x + 0.044715 * (x * x * x))))


def _layer_norm(x, w, b):
    mu = jnp.mean(x, axis=-1, keepdims=True)
    xc = x - mu
    var = jnp.mean(xc * xc, axis=-1, keepdims=True)
    return xc * lax.rsqrt(var + LN_EPS) * w + b


def _norm_rows(x):
    mu = jnp.mean(x, axis=0, keepdims=True)
    xc = x - mu
    var = jnp.mean(xc * xc, axis=0, keepdims=True)
    return xc * lax.rsqrt(var + LN_EPS)


def _lanes(col, n):
    return jnp.concatenate([col] * (n // V7X_LANES), axis=1)


def _const_spec(shape):
    nd = len(shape)
    return pl.BlockSpec(shape, lambda *_: (0,) * nd, pipeline_mode=pl.Buffered(1))


def _params(n_axes):
    return pltpu.CompilerParams(dimension_semantics=("arbitrary",) * n_axes,
                                vmem_limit_bytes=V7X_VMEM_LIMIT_BYTES)


def _mod_kernel(c_ref, w_ref, b_ref, o_ref):
    c = c_ref[...]
    s = (c * _sigmoid(c)).astype(BF16)
    o_ref[0] = _dot(s, w_ref[0]) + b_ref[0]


def _modulation(c, ada_w, ada_b):
    n_layers = ada_w.shape[0]
    bsz = c.shape[0]
    mod = pl.pallas_call(
        _mod_kernel,
        out_shape=jax.ShapeDtypeStruct((n_layers * N_MOD, bsz, D_MODEL), F32),
        grid=(n_layers, N_MOD),
        in_specs=[pl.BlockSpec((bsz, D_MODEL), lambda l, j: (0, 0)),
                  pl.BlockSpec((1, D_MODEL, D_MODEL), lambda l, j: (l, 0, j)),
                  pl.BlockSpec((1, 1, D_MODEL), lambda l, j: (l, 0, j))],
        out_specs=pl.BlockSpec((1, bsz, D_MODEL), lambda l, j: (l * N_MOD + j, 0, 0)),
        compiler_params=_params(2),
        name="adaln_mod",
    )(c, ada_w, ada_b)
    return mod.reshape(n_layers * N_MOD, bsz, 1, D_MODEL)


def _mod_spec(layer, k):
    return pl.BlockSpec((None, 1, 1, D_MODEL), lambda b, i: (layer * N_MOD + k, b, 0, 0))


def _pre_kernel(xp_ref, x_ref, xn_ref, sh_ref, sc_ref, w_ref, b_ref, wft_ref, bft_ref, cw_ref, cb_ref,
                wqt_ref, wk_ref, wvt_ref, qt_ref, k_ref, vt_ref, gt_ref, ak_ref, avt_ref, xs_scr, xm_scr):
    t = x_ref.shape[1]
    i = pl.program_id(1)
    last = pl.num_programs(1) - 1
    scale = 1.0 + sc_ref[0]
    shift = sh_ref[0]
    xs_scr[0:HALO, :] = xp_ref[0]
    xs_scr[HALO:HALO + t, :] = x_ref[0]
    xs_scr[HALO + t:, :] = xn_ref[0]
    h_ext = (xs_scr[...] * scale + shift).astype(BF16)
    proj = _dot(h_ext, w_ref[...]) + b_ref[...]
    xm_scr[0:HALO, :] = jnp.where(i > 0, proj[0:HALO, :D_MODEL], 0.0)
    xm_scr[HALO:HALO + t, :] = proj[HALO:HALO + t, :D_MODEL]
    xm_scr[HALO + t:, :] = jnp.where(i < last, proj[HALO + t:, :D_MODEL], 0.0)
    ak_ref[0] = proj[HALO:HALO + t, D_MODEL:D_MODEL + A_KV].astype(BF16)
    h = (x_ref[0] * scale + shift).astype(BF16)
    ft = _dot_nt(wft_ref[...], h) + _lanes(bft_ref[...], t)
    avt_ref[0] = ft[:A_KV].astype(BF16)
    gt_ref[0] = ft[A_KV:]
    conv = cb_ref[...]
    for j in range(CONV_K):
        off = HALO - CONV_K // 2 + j
        conv = conv + cw_ref[j:j + 1, :] * xm_scr[off:off + t, :]
    hc = 0.5 * conv
    xc = (hc + hc * jnp.tanh(hc)).astype(BF16)
    xm = xm_scr[HALO:HALO + t, :].astype(BF16)
    for hd in range(M_HEADS):
        sl = slice(hd * M_DH, (hd + 1) * M_DH)
        qt_ref[0, sl, :] = _dot_nt(wqt_ref[hd], xc[:, sl]).astype(BF16)
        k_ref[0, :, sl] = (_dot(xc[:, sl], wk_ref[hd]) * (M_DH ** -0.5)).astype(BF16)
        vt_ref[0, sl, :] = _dot_nt(wvt_ref[hd], xm[:, sl]).astype(BF16)


def _pre_call(x, mod, layer, w):
    bsz, seq, _ = x.shape
    t = min(WIDE_TILE, seq)
    nt = seq // t
    hb = t // HALO
    nhb = seq // HALO
    tok = lambda n: pl.BlockSpec((1, t, n), lambda b, i: (b, i, 0))
    feat = lambda n: pl.BlockSpec((1, n, t), lambda b, i: (b, 0, i))
    tok_shape = lambda n, dt: jax.ShapeDtypeStruct((bsz, seq, n), dt)
    feat_shape = lambda n, dt: jax.ShapeDtypeStruct((bsz, n, seq), dt)
    return pl.pallas_call(
        _pre_kernel,
        out_shape=(feat_shape(D_MODEL, BF16), tok_shape(D_MODEL, BF16), feat_shape(D_MODEL, BF16),
                   feat_shape(4 * M_HEADS, F32), tok_shape(A_KV, BF16), feat_shape(A_KV, BF16)),
        grid=(bsz, nt),
        in_specs=[pl.BlockSpec((1, HALO, D_MODEL), lambda b, i: (b, jnp.maximum(i * hb - 1, 0), 0)),
                  tok(D_MODEL),
                  pl.BlockSpec((1, HALO, D_MODEL), lambda b, i: (b, jnp.minimum((i + 1) * hb, nhb - 1), 0)),
                  _mod_spec(layer, 0), _mod_spec(layer, 1),
                  _const_spec((D_MODEL, N_PRE)), _const_spec((1, N_PRE)),
                  _const_spec((N_FT, D_MODEL)), _const_spec((N_FT, V7X_LANES)),
                  _const_spec((CONV_K, D_MODEL)), _const_spec((1, D_MODEL)),
                  _const_spec((M_HEADS, M_DH, M_DH)), _const_spec((M_HEADS, M_DH, M_DH)),
                  _const_spec((M_HEADS, M_DH, M_DH))],
        out_specs=(feat(D_MODEL), tok(D_MODEL), feat(D_MODEL), feat(4 * M_HEADS), tok(A_KV), feat(A_KV)),
        scratch_shapes=[pltpu.VMEM((t + 2 * HALO, D_MODEL), F32), pltpu.VMEM((t + 2 * HALO, D_MODEL), F32)],
        compiler_params=_params(2),
        name="pre_proj_conv_qkv",
    )(x, x, x, mod, mod, w["w_pre"], w["b_pre"], w["w_ft"], w["b_ft"], w["conv_w"], w["conv_b"],
      w["wqt"], w["wk"], w["wvt"])


def _scan_kernel(qt_ref, k_ref, vt_ref, gr_ref, o_ref, hs_scr, c_scr, sc_scr):
    seq = k_ref.shape[1]
    lc = gr_ref.shape[4]
    nc = seq // lc
    nh = gr_ref.shape[2]
    si = lax.broadcasted_iota(jnp.int32, (lc, lc), 0)
    li = lax.broadcasted_iota(jnp.int32, (lc, lc), 1)
    masks = (si <= li, si >= li)
    ones_rows = jnp.ones((AUG, lc), BF16)
    c_scr[...] = jnp.zeros_like(c_scr)

    gate = {}
    for hd in range(nh):
        for d in range(2):
            ig = gr_ref[0, 2 * d, hd] * LOG2E
            lf = _log_sigmoid(gr_ref[0, 2 * d + 1, hd]) * LOG2E
            a = jnp.dot(lf, masks[d].astype(F32), precision=lax.Precision.HIGHEST, preferred_element_type=F32)
            w = ig - a
            a_tot = a[:, lc - 1:lc] if d == 0 else a[:, 0:1]
            w_max = jnp.max(w, axis=1, keepdims=True)
            w_cols = jnp.concatenate([w, jnp.zeros((V7X_LANES - nc, lc), F32)], axis=0).T
            m = jnp.zeros((1, 1), F32)
            m_seq = []
            for step in range(nc):
                jc = step if d == 0 else nc - 1 - step
                m_next = a_tot[jc:jc + 1] + jnp.maximum(m, w_max[jc:jc + 1])
                m_seq.append((m, m_next))
                m = m_next
            gate[hd, d] = (a, w, a_tot, w_cols, m_seq)

    for hd in range(nh):
        rows = slice(hd * M_DH, (hd + 1) * M_DH)
        for jc in range(nc):
            cols = slice(jc * lc, (jc + 1) * lc)
            sc_scr[hd, jc] = _dot(k_ref[0, cols, rows], qt_ref[0, rows, cols])

    for step in range(nc):
        for hd in range(nh):
            rows = slice(hd * M_DH, (hd + 1) * M_DH)
            for d in range(2):
                a, w, a_tot, w_cols, m_seq = gate[hd, d]
                jc = step if d == 0 else nc - 1 - step
                cols = slice(jc * lc, (jc + 1) * lc)
                m, m_next = m_seq[step]
                qt = qt_ref[0, rows, cols]
                k = k_ref[0, cols, rows]
                vta = jnp.concatenate([vt_ref[0, rows, cols], ones_rows], axis=0)
                wm = jnp.where(masks[d], w_cols[:, jc:jc + 1], -jnp.inf)
                u = -jnp.maximum(m, jnp.max(wm, axis=0, keepdims=True))
                st = sc_scr[hd, jc] * jnp.exp2(wm + u)
                big = (_dot(vta, st.astype(BF16))
                       + _dot(c_scr[hd, d].astype(BF16), qt * jnp.exp2(m + u).astype(BF16)))
                den = jnp.maximum(jnp.abs(big[M_DH:M_DH + 1, :]), jnp.exp2(u - a[jc:jc + 1, :]))
                ht = big[:M_DH, :] * (1.0 / den)
                if step < nc // 2:
                    hs_scr[rows, cols] = ht
                else:
                    hs_scr[rows, cols] = hs_scr[rows, cols] + ht
                wg = jnp.exp2(a_tot[jc:jc + 1] + w[jc:jc + 1, :] - m_next)
                decay = jnp.exp2(a_tot[jc:jc + 1] + m - m_next)
                c_scr[hd, d] = decay * c_scr[hd, d] + _dot(vta * wg.astype(BF16), k)

    blk = min(512, seq)
    for hd in range(nh):
        rows = slice(hd * M_DH, (hd + 1) * M_DH)
        for r in range(seq // blk):
            cols = slice(r * blk, (r + 1) * blk)
            o_ref[0, rows, cols] = _norm_rows(hs_scr[rows, cols]).astype(o_ref.dtype)


def _scan_call(qt, k, vt, grow):
    bsz, seq, _ = k.shape
    nc, lc = grow.shape[3], grow.shape[4]
    nh = SCAN_HEADS
    feat = pl.BlockSpec((1, nh * M_DH, seq), lambda b, h: (b, h, 0))
    return pl.pallas_call(
        _scan_kernel,
        out_shape=jax.ShapeDtypeStruct((bsz, D_MODEL, seq), BF16),
        grid=(bsz, M_HEADS // nh),
        in_specs=[feat, pl.BlockSpec((1, seq, nh * M_DH), lambda b, h: (b, 0, h)), feat,
                  pl.BlockSpec((1, 4, nh, nc, lc), lambda b, h: (b, 0, h, 0, 0))],
        out_specs=feat,
        scratch_shapes=[pltpu.VMEM((nh * M_DH, seq), F32), pltpu.VMEM((nh, 2, M_DH + AUG, M_DH), F32),
                        pltpu.VMEM((nh, nc, lc, lc), F32)],
        compiler_params=_params(2),
        name="mlstm_scan",
    )(qt, k, vt, grow)


def _mix_kernel(x_ref, hnt_ref, kp_ref, k_ref, kn_ref, vtp_ref, vt_ref, vtn_ref, sh_ref, sc_ref, gm_ref, sink_ref,
                wt_ref, bt_ref, nw_ref, clw_ref, clb_ref, ws_ref, bs_ref, pmt_ref, pat_ref, pct_ref, wo_ref,
                l1w_ref, l1b_ref, o_ref, kx_scr, vtx_scr, yat_scr, yct_scr, s_scr):
    t = x_ref.shape[1]
    i = pl.program_id(1)
    last = pl.num_programs(1) - 1
    nqb = t // WINDOW
    x = x_ref[0]
    h = (x * (1.0 + sc_ref[0]) + sh_ref[0]).astype(BF16)

    def proj_t(off, n):
        return _dot_nt(wt_ref[off:off + n, :], h) + _lanes(bt_ref[off:off + n, :], t)

    p_zm = proj_t(MIX_ZM, D_MODEL)
    p_q = proj_t(MIX_Q, D_A)
    p_uv = proj_t(MIX_UV, 2 * D_C)
    p_g0 = proj_t(MIX_G, D_MODEL)
    p_g1 = proj_t(MIX_G + D_MODEL, D_MODEL)
    p_g2 = proj_t(MIX_G + 2 * D_MODEL, D_MODEL)

    y_mt = (hnt_ref[0].astype(F32) * _lanes(nw_ref[...], t) * _sigmoid(p_zm)).astype(BF16)
    acc = _sigmoid(p_g0) * _dot(pmt_ref[...], y_mt)

    kx_scr[0:WINDOW, :] = kp_ref[0]
    kx_scr[WINDOW:WINDOW + t, :] = k_ref[0]
    kx_scr[WINDOW + t:, :] = kn_ref[0]
    vtx_scr[:, 0:WINDOW] = vtp_ref[0]
    vtx_scr[:, WINDOW:WINDOW + t] = vt_ref[0]
    vtx_scr[:, WINDOW + t:] = vtn_ref[0]
    qt = (p_q * (LOG2E * A_DH ** -0.5)).astype(BF16)
    kj = lax.broadcasted_iota(jnp.int32, (3 * WINDOW, WINDOW), 0)
    qi = lax.broadcasted_iota(jnp.int32, (3 * WINDOW, WINDOW), 1)
    dist = jnp.abs(qi + WINDOW - kj)
    band = dist <= WINDOW
    distf = dist.astype(F32)
    first_pen = jnp.where(i == 0, -jnp.inf, 0.0)
    last_pen = jnp.where(i == last, -jnp.inf, 0.0)
    ones_rows = jnp.ones((AUG, 3 * WINDOW), BF16)
    for g in range(A_KV_HEADS):
        for n in range(nqb):
            qb = jnp.concatenate([qt[(g * A_REP + r) * A_DH:(g * A_REP + r + 1) * A_DH, n * WINDOW:(n + 1) * WINDOW]
                                  for r in range(A_REP)], axis=1)
            kb = kx_scr[n * WINDOW:(n + 3) * WINDOW, g * A_DH:(g + 1) * A_DH]
            s_scr[g * nqb + n] = _dot(kb, qb)
    for g in range(A_KV_HEADS):
        bias = jnp.concatenate(
            [jnp.where(band, -(LOG2E * 2.0 ** (-8.0 * (g * A_REP + r + 1) / A_HEADS)) * distf, -jnp.inf)
             for r in range(A_REP)], axis=1)
        sink = jnp.concatenate([jnp.full((1, WINDOW), sink_ref[g * A_REP + r] * LOG2E, F32) for r in range(A_REP)],
                               axis=1)
        for n in range(nqb):
            vtb = jnp.concatenate([vtx_scr[g * A_DH:(g + 1) * A_DH, n * WINDOW:(n + 3) * WINDOW], ones_rows], axis=0)
            blk_bias = bias
            if n == 0:
                blk_bias = jnp.concatenate([bias[:WINDOW] + first_pen, bias[WINDOW:]], axis=0)
            if n == nqb - 1:
                blk_bias = jnp.concatenate([blk_bias[:2 * WINDOW], blk_bias[2 * WINDOW:] + last_pen], axis=0)
            s = s_scr[g * nqb + n] + blk_bias
            mx = jnp.maximum(jnp.max(s, axis=0, keepdims=True), sink)
            p = jnp.exp2(s - mx).astype(BF16)
            ot = _dot(vtb, p)
            ot = ot[:A_DH] * (1.0 / (ot[A_DH:A_DH + 1] + jnp.exp2(sink - mx)))
            for r in range(A_REP):
                r0 = (g * A_REP + r) * A_DH
                yat_scr[r0:r0 + A_DH, n * WINDOW:(n + 1) * WINDOW] = ot[:, r * WINDOW:(r + 1) * WINDOW]
    acc = acc + _sigmoid(p_g1) * _dot(pat_ref[...], yat_scr[...].astype(BF16))

    uvt = _gelu_tanh(p_uv)
    vnt = (_norm_rows(uvt[D_C:]) * _lanes(clw_ref[...], t) + _lanes(clb_ref[...], t)).astype(BF16)
    gw = D_C // C_GROUPS
    for n in range(t // C_CHUNK):
        for g in range(C_GROUPS):
            vst = _dot_nt(vnt[g * gw:(g + 1) * gw, n * C_CHUNK:(n + 1) * C_CHUNK], ws_ref[g]) + bs_ref[g:g + 1, :]
            yct_scr[g * gw:(g + 1) * gw, n * C_CHUNK:(n + 1) * C_CHUNK] = vst
    y_ct = (uvt[:D_C] * yct_scr[...]).astype(BF16)
    acc = acc + _sigmoid(p_g2) * _dot(pct_ref[...], y_ct)

    mix = _dot_tn(acc.astype(BF16), wo_ref[...])
    o_ref[0] = _layer_norm(ALPHA * x + (1.0 + gm_ref[0]) * mix, l1w_ref[...], l1b_ref[...])


def _mix_call(x, hnt, ak, avt, mod, layer, w):
    bsz, seq, _ = x.shape
    t = min(TOKEN_TILE, seq)
    nt = seq // t
    wb = t // WINDOW
    nwb = seq // WINDOW
    tok = lambda n: pl.BlockSpec((1, t, n), lambda b, i: (b, i, 0))
    feat = lambda n: pl.BlockSpec((1, n, t), lambda b, i: (b, 0, i))
    prev_blk = lambda b, i: jnp.maximum(i * wb - 1, 0)
    next_blk = lambda b, i: jnp.minimum((i + 1) * wb, nwb - 1)
    return pl.pallas_call(
        _mix_kernel,
        out_shape=jax.ShapeDtypeStruct((bsz, seq, D_MODEL), F32),
        grid=(bsz, nt),
        in_specs=[tok(D_MODEL), feat(D_MODEL),
                  pl.BlockSpec((1, WINDOW, A_KV), lambda b, i: (b, prev_blk(b, i), 0)),
                  tok(A_KV),
                  pl.BlockSpec((1, WINDOW, A_KV), lambda b, i: (b, next_blk(b, i), 0)),
                  pl.BlockSpec((1, A_KV, WINDOW), lambda b, i: (b, 0, prev_blk(b, i))),
                  feat(A_KV),
                  pl.BlockSpec((1, A_KV, WINDOW), lambda b, i: (b, 0, next_blk(b, i))),
                  _mod_spec(layer, 0), _mod_spec(layer, 1), _mod_spec(layer, 2),
                  pl.BlockSpec(memory_space=pltpu.SMEM),
                  _const_spec((N_MIX, D_MODEL)), _const_spec((N_MIX, V7X_LANES)),
                  _const_spec((D_MODEL, V7X_LANES)), _const_spec((D_C, V7X_LANES)), _const_spec((D_C, V7X_LANES)),
                  _const_spec((C_GROUPS, C_CHUNK, C_CHUNK)), _const_spec((C_GROUPS, C_CHUNK)),
                  _const_spec((D_MODEL, D_MODEL)), _const_spec((D_MODEL, D_A)), _const_spec((D_MODEL, D_C)),
                  _const_spec((D_MODEL, D_MODEL)), _const_spec((1, D_MODEL)), _const_spec((1, D_MODEL))],
        out_specs=tok(D_MODEL),
        scratch_shapes=[pltpu.VMEM((t + 2 * WINDOW, A_KV), BF16), pltpu.VMEM((A_KV, t + 2 * WINDOW), BF16),
                        pltpu.VMEM((D_A, t), F32), pltpu.VMEM((D_C, t), F32),
                        pltpu.VMEM((A_KV_HEADS * (t // WINDOW), 3 * WINDOW, A_REP * WINDOW), F32)],
        compiler_params=_params(2),
        name="mix_merge_ln1",
    )(x, hnt, ak, ak, ak, avt, avt, avt, mod, mod, mod, w["sink"], w["w_mixt"], w["b_mixt"], w["norm_w"],
      w["c_ln_w"], w["c_ln_b"], w["c_ws"], w["c_bs"], w["p_mt"], w["p_at"], w["p_ct"], w["w_out"],
      w["ln1_w"], w["ln1_b"])


def _mlp_kernel(x_ref, sh_ref, sc_ref, gm_ref, w1_ref, b1_ref, w2_ref, b2_ref, lw_ref, lb_ref, o_ref):
    x = x_ref[0]
    h = (x * (1.0 + sc_ref[0]) + sh_ref[0]).astype(BF16)
    ff = b2_ref[...]
    for c in range(D_FF // FF_CHUNK):
        cs = slice(c * FF_CHUNK, (c + 1) * FF_CHUNK)
        hid = jnp.maximum(_dot(h, w1_ref[:, cs]) + b1_ref[:, cs], 0.0)
        ff = ff + _dot((hid * hid).astype(BF16), w2_ref[cs, :])
    o_ref[0] = _layer_norm(ALPHA * x + (1.0 + gm_ref[0]) * ff, lw_ref[...], lb_ref[...])


def _mlp_call(x, mod, layer, w):
    bsz, seq, _ = x.shape
    t = min(WIDE_TILE, seq)
    tile = pl.BlockSpec((1, t, D_MODEL), lambda b, i: (b, i, 0))
    return pl.pallas_call(
        _mlp_kernel,
        out_shape=jax.ShapeDtypeStruct((bsz, seq, D_MODEL), F32),
        grid=(bsz, seq // t),
        in_specs=[tile, _mod_spec(layer, 3), _mod_spec(layer, 4), _mod_spec(layer, 5),
                  _const_spec((D_MODEL, D_FF)), _const_spec((1, D_FF)),
                  _const_spec((D_FF, D_MODEL)), _const_spec((1, D_MODEL)),
                  _const_spec((1, D_MODEL)), _const_spec((1, D_MODEL))],
        out_specs=tile,
        compiler_params=_params(2),
        name="mlp_ln2",
    )(x, mod, mod, mod, w["w1"], w["b1"], w["w2"], w["b2"], w["ln2_w"], w["ln2_b"])


def _layer_weights(p, l):
    w_in, b_in = p["w_in"][l], p["b_in"][l]
    row = lambda name: p[name][l][None, :]
    col = lambda v: jnp.broadcast_to(v[:, None], (v.shape[0], V7X_LANES))
    w_pre = jnp.concatenate([w_in[:, OFF_MX:OFF_MZ], w_in[:, OFF_AK:OFF_AV]], axis=1)
    b_pre = jnp.concatenate([b_in[OFF_MX:OFF_MZ], b_in[OFF_AK:OFF_AV]])
    w_ft = jnp.concatenate([w_in[:, OFF_AV:OFF_C], w_in[:, OFF_MG:OFF_AQ]], axis=1)
    b_ft = jnp.concatenate([b_in[OFF_AV:OFF_C], b_in[OFF_MG:OFF_AQ]])
    w_mix = jnp.concatenate([w_in[:, OFF_MZ:OFF_MG], w_in[:, OFF_AQ:OFF_AK], w_in[:, OFF_C:]], axis=1)
    b_mix = jnp.concatenate([b_in[OFF_MZ:OFF_MG], b_in[OFF_AQ:OFF_AK], b_in[OFF_C:]])
    return dict(
        w_pre=w_pre.astype(BF16), b_pre=b_pre[None, :],
        w_ft=w_ft.T.astype(BF16), b_ft=col(b_ft),
        w_mixt=w_mix.T.astype(BF16), b_mixt=col(b_mix),
        conv_w=p["m_conv_w"][l], conv_b=row("m_conv_b"),
        wqt=jnp.transpose(p["m_wq"][l], (0, 2, 1)).astype(BF16), wk=p["m_wk"][l].astype(BF16),
        wvt=jnp.transpose(p["m_wv"][l], (0, 2, 1)).astype(BF16),
        norm_w=col(p["m_norm_w"][l]), sink=p["a_sink"][l],
        c_ln_w=col(p["c_ln_w"][l]), c_ln_b=col(p["c_ln_b"][l]),
        c_ws=p["c_ws"][l].astype(BF16), c_bs=p["c_bs"][l],
        p_mt=p["p_m"][l].T.astype(BF16), p_at=p["p_a"][l].T.astype(BF16), p_ct=p["p_c"][l].T.astype(BF16),
        w_out=p["w_out"][l].astype(BF16), ln1_w=row("ln1_w"), ln1_b=row("ln1_b"),
        w1=p["mlp_w1"][l].astype(BF16), b1=row("mlp_b1"), w2=p["mlp_w2"][l].astype(BF16), b2=row("mlp_b2"),
        ln2_w=row("ln2_w"), ln2_b=row("ln2_b"))


def _trunk(x, c, ada_w, ada_b, weights):
    bsz, seq, _ = x.shape
    lc = min(SCAN_CHUNK, seq)
    nc = seq // lc
    mod = _modulation(c, ada_w, ada_b)
    for l in range(DEPTH):
        w = weights[l]
        qt, k, vt, gt, ak, avt = _pre_call(x, mod, l, w)
        hnt = _scan_call(qt, k, vt, gt.reshape(bsz, 4, M_HEADS, nc, lc))
        x = _mix_call(x, hnt, ak, avt, mod, l, w)
        x = _mlp_call(x, mod, l, w)
    return x


def kernel(x_prompt, x_sample, c_prompt, c_sample, ada_w, ada_b, w_in, b_in, m_conv_w, m_conv_b, m_wq, m_wk, m_wv,
           m_norm_w, a_sink, c_ln_w, c_ln_b, c_ws, c_bs, p_m, p_a, p_c, w_out, ln1_w, ln1_b, mlp_w1, mlp_b1,
           mlp_w2, mlp_b2, ln2_w, ln2_b):
    p = dict(w_in=w_in, b_in=b_in, m_conv_w=m_conv_w, m_conv_b=m_conv_b, m_wq=m_wq,
             m_wk=m_wk, m_wv=m_wv, m_norm_w=m_norm_w, a_sink=a_sink, c_ln_w=c_ln_w, c_ln_b=c_ln_b, c_ws=c_ws,
             c_bs=c_bs, p_m=p_m, p_a=p_a, p_c=p_c, w_out=w_out, ln1_w=ln1_w, ln1_b=ln1_b, mlp_w1=mlp_w1,
             mlp_b1=mlp_b1, mlp_w2=mlp_w2, mlp_b2=mlp_b2, ln2_w=ln2_w, ln2_b=ln2_b)
    weights = [_layer_weights(p, l) for l in range(DEPTH)]
    ada_wb = ada_w.astype(BF16)
    ada_b3 = ada_b.reshape(DEPTH, 1, -1)
    return (_trunk(x_prompt, c_prompt, ada_wb, ada_b3, weights), _trunk(x_sample, c_sample, ada_wb, ada_b3, weights))
```

```python
import jax
import jax.numpy as jnp
from jax import lax
from jax.experimental import pallas as pl
from jax.experimental.pallas import tpu as pltpu

F32 = jnp.float32
BF16 = jnp.bfloat16

D_MODEL = 1024
M_HEADS = 4
M_DH = 256
CONV_K = 5
A_HEADS = 8
A_KV_HEADS = 2
A_DH = 64
A_REP = A_HEADS // A_KV_HEADS
D_A = A_HEADS * A_DH
A_KV = A_KV_HEADS * A_DH
WINDOW = 128
D_C = 512
C_GROUPS = 4
C_CHUNK = 128
D_FF = 4096
N_BRANCH = 3
DEPTH = 2
ALPHA = (2 * DEPTH) ** 0.25
LN_EPS = 1e-5

OFF_MX = 0
OFF_MZ = OFF_MX + D_MODEL
OFF_MG = OFF_MZ + D_MODEL
OFF_AQ = OFF_MG + 4 * M_HEADS
OFF_AK = OFF_AQ + D_A
OFF_AV = OFF_AK + A_KV
OFF_C = OFF_AV + A_KV
OFF_G = OFF_C + 2 * D_C

V7X_LANES = 128
V7X_SUBLANES_F32 = 8
V7X_SUBLANES_BF16 = 16
V7X_VMEM_BYTES = 64 * 1024 * 1024
V7X_VMEM_LIMIT_BYTES = V7X_VMEM_BYTES - 8 * 1024 * 1024
V7X_VMEM_LIMIT_SCAN_BYTES = V7X_VMEM_BYTES - 4 * 1024 * 1024

SCAN_CHUNK = 256
TOKEN_TILE = 512
WIDE_TILE = 1024
HALO = V7X_SUBLANES_F32
FF_CHUNK = 1024
N_MOD = 6
SCAN_HEADS = 4
LOG2E = 1.4426950408889634
AUG = V7X_SUBLANES_BF16
N_PRE = D_MODEL + A_KV
N_FT = A_KV + 4 * M_HEADS
N_MIX = D_MODEL + D_A + 2 * D_C + N_BRANCH * D_MODEL
MIX_ZM = 0
MIX_Q = MIX_ZM + D_MODEL
MIX_UV = MIX_Q + D_A
MIX_G = MIX_UV + 2 * D_C


def _dot(a, b):
    return jnp.dot(a, b, preferred_element_type=F32)


def _dot_nt(a, b):
    return lax.dot_general(a, b, (((1,), (1,)), ((), ())), preferred_element_type=F32)


def _dot_tn(a, b):
    return lax.dot_general(a, b, (((0,), (0,)), ((), ())), preferred_element_type=F32)


def _sigmoid(x):
    return 0.5 * (1.0 + jnp.tanh(0.5 * x))


def _log_sigmoid(x):
    return jnp.minimum(x, 0.0) - jnp.log(1.0 + jnp.exp(-jnp.abs(x)))


def _gelu_tanh(x):
    return 0.5 * x * (1.0 + jnp.tanh(0.7978845608028654 * (x + 0.044715 * (x * x * x))))


def _layer_norm(x, w, b):
    mu = jnp.mean(x, axis=-1, keepdims=True)
    xc = x - mu
    var = jnp.mean(xc * xc, axis=-1, keepdims=True)
    return xc * lax.rsqrt(var + LN_EPS) * w + b


def _norm_rows(x):
    mu = jnp.mean(x, axis=0, keepdims=True)
    xc = x - mu
    var = jnp.mean(xc * xc, axis=0, keepdims=True)
    return xc * lax.rsqrt(var + LN_EPS)


def _lanes(col, n):
    return jnp.concatenate([col] * (n // V7X_LANES), axis=1)


def _const_spec(shape):
    nd = len(shape)
    return pl.BlockSpec(shape, lambda *_: (0,) * nd, pipeline_mode=pl.Buffered(1))


def _params(n_axes, vmem_limit_bytes=V7X_VMEM_LIMIT_BYTES):
    return pltpu.CompilerParams(dimension_semantics=("arbitrary",) * n_axes, vmem_limit_bytes=vmem_limit_bytes)


def _mod_kernel(c_ref, w_ref, b_ref, o_ref):
    c = c_ref[...]
    s = (c * _sigmoid(c)).astype(BF16)
    o_ref[0] = _dot(s, w_ref[0]) + b_ref[0]


def _modulation(c, ada_w, ada_b):
    n_layers = ada_w.shape[0]
    bsz = c.shape[0]
    mod = pl.pallas_call(
        _mod_kernel,
        out_shape=jax.ShapeDtypeStruct((n_layers * N_MOD, bsz, D_MODEL), F32),
        grid=(n_layers, N_MOD),
        in_specs=[pl.BlockSpec((bsz, D_MODEL), lambda l, j: (0, 0)),
                  pl.BlockSpec((1, D_MODEL, D_MODEL), lambda l, j: (l, 0, j)),
                  pl.BlockSpec((1, 1, D_MODEL), lambda l, j: (l, 0, j))],
        out_specs=pl.BlockSpec((1, bsz, D_MODEL), lambda l, j: (l * N_MOD + j, 0, 0)),
        compiler_params=_params(2),
        name="adaln_mod",
    )(c, ada_w, ada_b)
    return mod.reshape(n_layers * N_MOD, bsz, 1, D_MODEL)


def _mod_spec(layer, k):
    return pl.BlockSpec((None, 1, 1, D_MODEL), lambda b, i: (layer * N_MOD + k, b, 0, 0))


def _pre_kernel(xp_ref, x_ref, xn_ref, sh_ref, sc_ref, w_ref, b_ref, wft_ref, bft_ref, cw_ref, cb_ref,
                wqt_ref, wk_ref, wvt_ref, qt_ref, k_ref, vt_ref, gt_ref, ak_ref, avt_ref, xs_scr, xm_scr):
    t = x_ref.shape[1]
    i = pl.program_id(1)
    last = pl.num_programs(1) - 1
    scale = 1.0 + sc_ref[0]
    shift = sh_ref[0]
    xs_scr[0:HALO, :] = xp_ref[0]
    xs_scr[HALO:HALO + t, :] = x_ref[0]
    xs_scr[HALO + t:, :] = xn_ref[0]
    h_ext = (xs_scr[...] * scale + shift).astype(BF16)
    proj = _dot(h_ext, w_ref[...]) + b_ref[...]
    xm_scr[0:HALO, :] = jnp.where(i > 0, proj[0:HALO, :D_MODEL], 0.0)
    xm_scr[HALO:HALO + t, :] = proj[HALO:HALO + t, :D_MODEL]
    xm_scr[HALO + t:, :] = jnp.where(i < last, proj[HALO + t:, :D_MODEL], 0.0)
    ak_ref[0] = proj[HALO:HALO + t, D_MODEL:D_MODEL + A_KV].astype(BF16)
    h = (x_ref[0] * scale + shift).astype(BF16)
    ft = _dot_nt(wft_ref[...], h) + _lanes(bft_ref[...], t)
    avt_ref[0] = ft[:A_KV].astype(BF16)
    gt_ref[0] = ft[A_KV:]
    conv = cb_ref[...]
    for j in range(CONV_K):
        off = HALO - CONV_K // 2 + j
        conv = conv + cw_ref[j:j + 1, :] * xm_scr[off:off + t, :]
    hc = 0.5 * conv
    xc = (hc + hc * jnp.tanh(hc)).astype(BF16)
    xm = xm_scr[HALO:HALO + t, :].astype(BF16)
    for hd in range(M_HEADS):
        sl = slice(hd * M_DH, (hd + 1) * M_DH)
        qt_ref[0, sl, :] = _dot_nt(wqt_ref[hd], xc[:, sl]).astype(BF16)
        k_ref[0, :, sl] = (_dot(xc[:, sl], wk_ref[hd]) * (M_DH ** -0.5)).astype(BF16)
        vt_ref[0, sl, :] = _dot_nt(wvt_ref[hd], xm[:, sl]).astype(BF16)


def _pre_call(x, mod, layer, w):
    bsz, seq, _ = x.shape
    t = min(WIDE_TILE, seq)
    nt = seq // t
    hb = t // HALO
    nhb = seq // HALO
    tok = lambda n: pl.BlockSpec((1, t, n), lambda b, i: (b, i, 0))
    feat = lambda n: pl.BlockSpec((1, n, t), lambda b, i: (b, 0, i))
    tok_shape = lambda n, dt: jax.ShapeDtypeStruct((bsz, seq, n), dt)
    feat_shape = lambda n, dt: jax.ShapeDtypeStruct((bsz, n, seq), dt)
    return pl.pallas_call(
        _pre_kernel,
        out_shape=(feat_shape(D_MODEL, BF16), tok_shape(D_MODEL, BF16), feat_shape(D_MODEL, BF16),
                   feat_shape(4 * M_HEADS, F32), tok_shape(A_KV, BF16), feat_shape(A_KV, BF16)),
        grid=(bsz, nt),
        in_specs=[pl.BlockSpec((1, HALO, D_MODEL), lambda b, i: (b, jnp.maximum(i * hb - 1, 0), 0)),
                  tok(D_MODEL),
                  pl.BlockSpec((1, HALO, D_MODEL), lambda b, i: (b, jnp.minimum((i + 1) * hb, nhb - 1), 0)),
                  _mod_spec(layer, 0), _mod_spec(layer, 1),
                  _const_spec((D_MODEL, N_PRE)), _const_spec((1, N_PRE)),
                  _const_spec((N_FT, D_MODEL)), _const_spec((N_FT, V7X_LANES)),
                  _const_spec((CONV_K, D_MODEL)), _const_spec((1, D_MODEL)),
                  _const_spec((M_HEADS, M_DH, M_DH)), _const_spec((M_HEADS, M_DH, M_DH)),
                  _const_spec((M_HEADS, M_DH, M_DH))],
        out_specs=(feat(D_MODEL), tok(D_MODEL), feat(D_MODEL), feat(4 * M_HEADS), tok(A_KV), feat(A_KV)),
        scratch_shapes=[pltpu.VMEM((t + 2 * HALO, D_MODEL), F32), pltpu.VMEM((t + 2 * HALO, D_MODEL), F32)],
        compiler_params=_params(2),
        name="pre_proj_conv_qkv",
    )(x, x, x, mod, mod, w["w_pre"], w["b_pre"], w["w_ft"], w["b_ft"], w["conv_w"], w["conv_b"],
      w["wqt"], w["wk"], w["wvt"])


def _scan_kernel(qt_ref, k_ref, vt_ref, gr_ref, o_ref, hs_scr, c_scr, sc_scr):
    seq = k_ref.shape[1]
    lc = gr_ref.shape[4]
    nc = seq // lc
    nh = gr_ref.shape[2]
    si = lax.broadcasted_iota(jnp.int32, (lc, lc), 0)
    li = lax.broadcasted_iota(jnp.int32, (lc, lc), 1)
    masks = (si <= li, si >= li)
    ones_rows = jnp.ones((AUG, lc), BF16)
    c_scr[...] = jnp.zeros_like(c_scr)

    gate = {}
    for hd in range(nh):
        for d in range(2):
            ig = gr_ref[0, 2 * d, hd] * LOG2E
            lf = _log_sigmoid(gr_ref[0, 2 * d + 1, hd]) * LOG2E
            a = jnp.dot(lf, masks[d].astype(F32), precision=lax.Precision.HIGHEST, preferred_element_type=F32)
            w = ig - a
            a_tot = a[:, lc - 1:lc] if d == 0 else a[:, 0:1]
            w_max = jnp.max(w, axis=1, keepdims=True)
            w_cols = jnp.concatenate([w, jnp.zeros((V7X_LANES - nc, lc), F32)], axis=0).T
            m = jnp.zeros((1, 1), F32)
            m_seq = []
            for step in range(nc):
                jc = step if d == 0 else nc - 1 - step
                m_next = a_tot[jc:jc + 1] + jnp.maximum(m, w_max[jc:jc + 1])
                m_seq.append((m, m_next))
                m = m_next
            gate[hd, d] = (a, w, a_tot, w_cols, m_seq)

    for hd in range(nh):
        rows = slice(hd * M_DH, (hd + 1) * M_DH)
        for jc in range(nc):
            cols = slice(jc * lc, (jc + 1) * lc)
            sc_scr[hd, jc] = _dot(k_ref[0, cols, rows], qt_ref[0, rows, cols])

    for step in range(nc):
        for hd in range(nh):
            rows = slice(hd * M_DH, (hd + 1) * M_DH)
            for d in range(2):
                a, w, a_tot, w_cols, m_seq = gate[hd, d]
                jc = step if d == 0 else nc - 1 - step
                cols = slice(jc * lc, (jc + 1) * lc)
                m, m_next = m_seq[step]
                qt = qt_ref[0, rows, cols]
                k = k_ref[0, cols, rows]
                vta = jnp.concatenate([vt_ref[0, rows, cols], ones_rows], axis=0)
                wm = jnp.where(masks[d], w_cols[:, jc:jc + 1], -jnp.inf)
                u = -jnp.maximum(m, jnp.max(wm, axis=0, keepdims=True))
                st = sc_scr[hd, jc] * jnp.exp2(wm + u)
                big = (_dot(vta, st.astype(BF16))
                       + _dot(c_scr[hd, d].astype(BF16), qt * jnp.exp2(m + u).astype(BF16)))
                den = jnp.maximum(jnp.abs(big[M_DH:M_DH + 1, :]), jnp.exp2(u - a[jc:jc + 1, :]))
                ht = big[:M_DH, :] * (1.0 / den)
                if step < nc // 2:
                    hs_scr[rows, cols] = ht
                else:
                    hs_scr[rows, cols] = hs_scr[rows, cols] + ht
                wg = jnp.exp2(a_tot[jc:jc + 1] + w[jc:jc + 1, :] - m_next)
                decay = jnp.exp2(a_tot[jc:jc + 1] + m - m_next)
                c_scr[hd, d] = decay * c_scr[hd, d] + _dot(vta * wg.astype(BF16), k)

    blk = min(512, seq)
    for hd in range(nh):
        rows = slice(hd * M_DH, (hd + 1) * M_DH)
        for r in range(seq // blk):
            cols = slice(r * blk, (r + 1) * blk)
            o_ref[0, rows, cols] = _norm_rows(hs_scr[rows, cols]).astype(o_ref.dtype)


def _scan_call(qt, k, vt, grow):
    bsz, seq, _ = k.shape
    nc, lc = grow.shape[3], grow.shape[4]
    nh = SCAN_HEADS
    feat = pl.BlockSpec((1, nh * M_DH, seq), lambda b, h: (b, h, 0))
    return pl.pallas_call(
        _scan_kernel,
        out_shape=jax.ShapeDtypeStruct((bsz, D_MODEL, seq), BF16),
        grid=(bsz, M_HEADS // nh),
        in_specs=[feat, pl.BlockSpec((1, seq, nh * M_DH), lambda b, h: (b, 0, h)), feat,
                  pl.BlockSpec((1, 4, nh, nc, lc), lambda b, h: (b, 0, h, 0, 0))],
        out_specs=feat,
        scratch_shapes=[pltpu.VMEM((nh * M_DH, seq), F32), pltpu.VMEM((nh, 2, M_DH + AUG, M_DH), F32),
                        pltpu.VMEM((nh, nc, lc, lc), F32)],
        compiler_params=_params(2, V7X_VMEM_LIMIT_SCAN_BYTES),
        name="mlstm_scan",
    )(qt, k, vt, grow)


def _mix_kernel(x_ref, hnt_ref, kp_ref, k_ref, kn_ref, vtp_ref, vt_ref, vtn_ref, sh_ref, sc_ref, gm_ref, sink_ref,
                wt_ref, bt_ref, nw_ref, clw_ref, clb_ref, ws_ref, bs_ref, pmt_ref, pat_ref, pct_ref, wo_ref,
                l1w_ref, l1b_ref, o_ref, kx_scr, vtx_scr, yat_scr, yct_scr, s_scr):
    t = x_ref.shape[1]
    i = pl.program_id(1)
    last = pl.num_programs(1) - 1
    nqb = t // WINDOW
    x = x_ref[0]
    h = (x * (1.0 + sc_ref[0]) + sh_ref[0]).astype(BF16)

    def proj_t(off, n):
        return _dot_nt(wt_ref[off:off + n, :], h) + _lanes(bt_ref[off:off + n, :], t)

    p_zm = proj_t(MIX_ZM, D_MODEL)
    p_q = proj_t(MIX_Q, D_A)
    p_uv = proj_t(MIX_UV, 2 * D_C)
    p_g0 = proj_t(MIX_G, D_MODEL)
    p_g1 = proj_t(MIX_G + D_MODEL, D_MODEL)
    p_g2 = proj_t(MIX_G + 2 * D_MODEL, D_MODEL)

    y_mt = (hnt_ref[0].astype(F32) * _lanes(nw_ref[...], t) * _sigmoid(p_zm)).astype(BF16)
    acc = _sigmoid(p_g0) * _dot(pmt_ref[...], y_mt)

    kx_scr[0:WINDOW, :] = kp_ref[0]
    kx_scr[WINDOW:WINDOW + t, :] = k_ref[0]
    kx_scr[WINDOW + t:, :] = kn_ref[0]
    vtx_scr[:, 0:WINDOW] = vtp_ref[0]
    vtx_scr[:, WINDOW:WINDOW + t] = vt_ref[0]
    vtx_scr[:, WINDOW + t:] = vtn_ref[0]
    qt = (p_q * (LOG2E * A_DH ** -0.5)).astype(BF16)
    kj = lax.broadcasted_iota(jnp.int32, (3 * WINDOW, WINDOW), 0)
    qi = lax.broadcasted_iota(jnp.int32, (3 * WINDOW, WINDOW), 1)
    dist = jnp.abs(qi + WINDOW - kj)
    band = dist <= WINDOW
    distf = dist.astype(F32)
    first_pen = jnp.where(i == 0, -jnp.inf, 0.0)
    last_pen = jnp.where(i == last, -jnp.inf, 0.0)
    ones_rows = jnp.ones((AUG, 3 * WINDOW), BF16)
    for g in range(A_KV_HEADS):
        for n in range(nqb):
            qb = jnp.concatenate([qt[(g * A_REP + r) * A_DH:(g * A_REP + r + 1) * A_DH, n * WINDOW:(n + 1) * WINDOW]
                                  for r in range(A_REP)], axis=1)
            kb = kx_scr[n * WINDOW:(n + 3) * WINDOW, g * A_DH:(g + 1) * A_DH]
            s_scr[g * nqb + n] = _dot(kb, qb)
    for g in range(A_KV_HEADS):
        bias = jnp.concatenate(
            [jnp.where(band, -(LOG2E * 2.0 ** (-8.0 * (g * A_REP + r + 1) / A_HEADS)) * distf, -jnp.inf)
             for r in range(A_REP)], axis=1)
        sink = jnp.concatenate([jnp.full((1, WINDOW), sink_ref[g * A_REP + r] * LOG2E, F32) for r in range(A_REP)],
                               axis=1)
        for n in range(nqb):
            vtb = jnp.concatenate([vtx_scr[g * A_DH:(g + 1) * A_DH, n * WINDOW:(n + 3) * WINDOW], ones_rows], axis=0)
            blk_bias = bias
            if n == 0:
                blk_bias = jnp.concatenate([bias[:WINDOW] + first_pen, bias[WINDOW:]], axis=0)
            if n == nqb - 1:
                blk_bias = jnp.concatenate([blk_bias[:2 * WINDOW], blk_bias[2 * WINDOW:] + last_pen], axis=0)
            s = s_scr[g * nqb + n] + blk_bias
            mx = jnp.maximum(jnp.max(s, axis=0, keepdims=True), sink)
            p = jnp.exp2(s - mx).astype(BF16)
            ot = _dot(vtb, p)
            ot = ot[:A_DH] * (1.0 / (ot[A_DH:A_DH + 1] + jnp.exp2(sink - mx)))
            for r in range(A_REP):
                r0 = (g * A_REP + r) * A_DH
                yat_scr[r0:r0 + A_DH, n * WINDOW:(n + 1) * WINDOW] = ot[:, r * WINDOW:(r + 1) * WINDOW]
    acc = acc + _sigmoid(p_g1) * _dot(pat_ref[...], yat_scr[...].astype(BF16))

    uvt = _gelu_tanh(p_uv)
    vnt = (_norm_rows(uvt[D_C:]) * _lanes(clw_ref[...], t) + _lanes(clb_ref[...], t)).astype(BF16)
    gw = D_C // C_GROUPS
    for n in range(t // C_CHUNK):
        for g in range(C_GROUPS):
            vst = _dot_nt(vnt[g * gw:(g + 1) * gw, n * C_CHUNK:(n + 1) * C_CHUNK], ws_ref[g]) + bs_ref[g:g + 1, :]
            yct_scr[g * gw:(g + 1) * gw, n * C_CHUNK:(n + 1) * C_CHUNK] = vst
    y_ct = (uvt[:D_C] * yct_scr[...]).astype(BF16)
    acc = acc + _sigmoid(p_g2) * _dot(pct_ref[...], y_ct)

    mix = _dot_tn(acc.astype(BF16), wo_ref[...])
    o_ref[0] = _layer_norm(ALPHA * x + (1.0 + gm_ref[0]) * mix, l1w_ref[...], l1b_ref[...])


def _mix_call(x, hnt, ak, avt, mod, layer, w):
    bsz, seq, _ = x.shape
    t = min(TOKEN_TILE, seq)
    nt = seq // t
    wb = t // WINDOW
    nwb = seq // WINDOW
    tok = lambda n: pl.BlockSpec((1, t, n), lambda b, i: (b, i, 0))
    feat = lambda n: pl.BlockSpec((1, n, t), lambda b, i: (b, 0, i))
    prev_blk = lambda b, i: jnp.maximum(i * wb - 1, 0)
    next_blk = lambda b, i: jnp.minimum((i + 1) * wb, nwb - 1)
    return pl.pallas_call(
        _mix_kernel,
        out_shape=jax.ShapeDtypeStruct((bsz, seq, D_MODEL), F32),
        grid=(bsz, nt),
        in_specs=[tok(D_MODEL), feat(D_MODEL),
                  pl.BlockSpec((1, WINDOW, A_KV), lambda b, i: (b, prev_blk(b, i), 0)),
                  tok(A_KV),
                  pl.BlockSpec((1, WINDOW, A_KV), lambda b, i: (b, next_blk(b, i), 0)),
                  pl.BlockSpec((1, A_KV, WINDOW), lambda b, i: (b, 0, prev_blk(b, i))),
                  feat(A_KV),
                  pl.BlockSpec((1, A_KV, WINDOW), lambda b, i: (b, 0, next_blk(b, i))),
                  _mod_spec(layer, 0), _mod_spec(layer, 1), _mod_spec(layer, 2),
                  pl.BlockSpec(memory_space=pltpu.SMEM),
                  _const_spec((N_MIX, D_MODEL)), _const_spec((N_MIX, V7X_LANES)),
                  _const_spec((D_MODEL, V7X_LANES)), _const_spec((D_C, V7X_LANES)), _const_spec((D_C, V7X_LANES)),
                  _const_spec((C_GROUPS, C_CHUNK, C_CHUNK)), _const_spec((C_GROUPS, C_CHUNK)),
                  _const_spec((D_MODEL, D_MODEL)), _const_spec((D_MODEL, D_A)), _const_spec((D_MODEL, D_C)),
                  _const_spec((D_MODEL, D_MODEL)), _const_spec((1, D_MODEL)), _const_spec((1, D_MODEL))],
        out_specs=tok(D_MODEL),
        scratch_shapes=[pltpu.VMEM((t + 2 * WINDOW, A_KV), BF16), pltpu.VMEM((A_KV, t + 2 * WINDOW), BF16),
                        pltpu.VMEM((D_A, t), F32), pltpu.VMEM((D_C, t), F32),
                        pltpu.VMEM((A_KV_HEADS * (t // WINDOW), 3 * WINDOW, A_REP * WINDOW), F32)],
        compiler_params=_params(2),
        name="mix_merge_ln1",
    )(x, hnt, ak, ak, ak, avt, avt, avt, mod, mod, mod, w["sink"], w["w_mixt"], w["b_mixt"], w["norm_w"],
      w["c_ln_w"], w["c_ln_b"], w["c_ws"], w["c_bs"], w["p_mt"], w["p_at"], w["p_ct"], w["w_out"],
      w["ln1_w"], w["ln1_b"])


def _mlp_kernel(x_ref, sh_ref, sc_ref, gm_ref, w1_ref, b1_ref, w2_ref, b2_ref, lw_ref, lb_ref, o_ref):
    x = x_ref[0]
    h = (x * (1.0 + sc_ref[0]) + sh_ref[0]).astype(BF16)
    ff = b2_ref[...]
    for c in range(D_FF // FF_CHUNK):
        cs = slice(c * FF_CHUNK, (c + 1) * FF_CHUNK)
        hid = jnp.maximum(_dot(h, w1_ref[:, cs]) + b1_ref[:, cs], 0.0)
        ff = ff + _dot((hid * hid).astype(BF16), w2_ref[cs, :])
    o_ref[0] = _layer_norm(ALPHA * x + (1.0 + gm_ref[0]) * ff, lw_ref[...], lb_ref[...])


def _mlp_call(x, mod, layer, w):
    bsz, seq, _ = x.shape
    t = min(WIDE_TILE, seq)
    tile = pl.BlockSpec((1, t, D_MODEL), lambda b, i: (b, i, 0))
    return pl.pallas_call(
        _mlp_kernel,
        out_shape=jax.ShapeDtypeStruct((bsz, seq, D_MODEL), F32),
        grid=(bsz, seq // t),
        in_specs=[tile, _mod_spec(layer, 3), _mod_spec(layer, 4), _mod_spec(layer, 5),
                  _const_spec((D_MODEL, D_FF)), _const_spec((1, D_FF)),
                  _const_spec((D_FF, D_MODEL)), _const_spec((1, D_MODEL)),
                  _const_spec((1, D_MODEL)), _const_spec((1, D_MODEL))],
        out_specs=tile,
        compiler_params=_params(2),
        name="mlp_ln2",
    )(x, mod, mod, mod, w["w1"], w["b1"], w["w2"], w["b2"], w["ln2_w"], w["ln2_b"])


def _layer_weights(p, l):
    w_in, b_in = p["w_in"][l], p["b_in"][l]
    row = lambda name: p[name][l][None, :]
    col = lambda v: jnp.broadcast_to(v[:, None], (v.shape[0], V7X_LANES))
    w_pre = jnp.concatenate([w_in[:, OFF_MX:OFF_MZ], w_in[:, OFF_AK:OFF_AV]], axis=1)
    b_pre = jnp.concatenate([b_in[OFF_MX:OFF_MZ], b_in[OFF_AK:OFF_AV]])
    w_ft = jnp.concatenate([w_in[:, OFF_AV:OFF_C], w_in[:, OFF_MG:OFF_AQ]], axis=1)
    b_ft = jnp.concatenate([b_in[OFF_AV:OFF_C], b_in[OFF_MG:OFF_AQ]])
    w_mix = jnp.concatenate([w_in[:, OFF_MZ:OFF_MG], w_in[:, OFF_AQ:OFF_AK], w_in[:, OFF_C:]], axis=1)
    b_mix = jnp.concatenate([b_in[OFF_MZ:OFF_MG], b_in[OFF_AQ:OFF_AK], b_in[OFF_C:]])
    return dict(
        w_pre=w_pre.astype(BF16), b_pre=b_pre[None, :],
        w_ft=w_ft.T.astype(BF16), b_ft=col(b_ft),
        w_mixt=w_mix.T.astype(BF16), b_mixt=col(b_mix),
        conv_w=p["m_conv_w"][l], conv_b=row("m_conv_b"),
        wqt=jnp.transpose(p["m_wq"][l], (0, 2, 1)).astype(BF16), wk=p["m_wk"][l].astype(BF16),
        wvt=jnp.transpose(p["m_wv"][l], (0, 2, 1)).astype(BF16),
        norm_w=col(p["m_norm_w"][l]), sink=p["a_sink"][l],
        c_ln_w=col(p["c_ln_w"][l]), c_ln_b=col(p["c_ln_b"][l]),
        c_ws=p["c_ws"][l].astype(BF16), c_bs=p["c_bs"][l],
        p_mt=p["p_m"][l].T.astype(BF16), p_at=p["p_a"][l].T.astype(BF16), p_ct=p["p_c"][l].T.astype(BF16),
        w_out=p["w_out"][l].astype(BF16), ln1_w=row("ln1_w"), ln1_b=row("ln1_b"),
        w1=p["mlp_w1"][l].astype(BF16), b1=row("mlp_b1"), w2=p["mlp_w2"][l].astype(BF16), b2=row("mlp_b2"),
        ln2_w=row("ln2_w"), ln2_b=row("ln2_b"))


def _trunk(x, c, ada_w, ada_b, weights):
    bsz, seq, _ = x.shape
    lc = min(SCAN_CHUNK, seq)
    nc = seq // lc
    mod = _modulation(c, ada_w, ada_b)
    for l in range(DEPTH):
        w = weights[l]
        qt, k, vt, gt, ak, avt = _pre_call(x, mod, l, w)
        hnt = _scan_call(qt, k, vt, gt.reshape(bsz, 4, M_HEADS, nc, lc))
        x = _mix_call(x, hnt, ak, avt, mod, l, w)
        x = _mlp_call(x, mod, l, w)
    return x


def kernel(x_prompt, x_sample, c_prompt, c_sample, ada_w, ada_b, w_in, b_in, m_conv_w, m_conv_b, m_wq, m_wk, m_wv,
           m_norm_w, a_sink, c_ln_w, c_ln_b, c_ws, c_bs, p_m, p_a, p_c, w_out, ln1_w, ln1_b, mlp_w1, mlp_b1,
           mlp_w2, mlp_b2, ln2_w, ln2_b):
    p = dict(w_in=w_in, b_in=b_in, m_conv_w=m_conv_w, m_conv_b=m_conv_b, m_wq=m_wq,
             m_wk=m_wk, m_wv=m_wv, m_norm_w=m_norm_w, a_sink=a_sink, c_ln_w=c_ln_w, c_ln_b=c_ln_b, c_ws=c_ws,
             c_bs=c_bs, p_m=p_m, p_a=p_a, p_c=p_c, w_out=w_out, ln1_w=ln1_w, ln1_b=ln1_b, mlp_w1=mlp_w1,
             mlp_b1=mlp_b1, mlp_w2=mlp_w2, mlp_b2=mlp_b2, ln2_w=ln2_w, ln2_b=ln2_b)
    weights = [_layer_weights(p, l) for l in range(DEPTH)]
    ada_wb = ada_w.astype(BF16)
    ada_b3 = ada_b.reshape(DEPTH, 1, -1)
    return (_trunk(x_prompt, c_prompt, ada_wb, ada_b3, weights), _trunk(x_sample, c_sample, ada_wb, ada_b3, weights))
```

```python
import jax
import jax.numpy as jnp
from jax import lax
from jax.experimental import pallas as pl
from jax.experimental.pallas import tpu as pltpu

F32 = jnp.float32
BF16 = jnp.bfloat16

D_MODEL = 1024
M_HEADS = 4
M_DH = 256
CONV_K = 5
A_HEADS = 8
A_KV_HEADS = 2
A_DH = 64
A_REP = A_HEADS // A_KV_HEADS
D_A = A_HEADS * A_DH
A_KV = A_KV_HEADS * A_DH
WINDOW = 128
D_C = 512
C_GROUPS = 4
C_CHUNK = 128
D_FF = 4096
N_BRANCH = 3
DEPTH = 2
ALPHA = (2 * DEPTH) ** 0.25
LN_EPS = 1e-5

OFF_MX = 0
OFF_MZ = OFF_MX + D_MODEL
OFF_MG = OFF_MZ + D_MODEL
OFF_AQ = OFF_MG + 4 * M_HEADS
OFF_AK = OFF_AQ + D_A
OFF_AV = OFF_AK + A_KV
OFF_C = OFF_AV + A_KV
OFF_G = OFF_C + 2 * D_C

V7X_LANES = 128
V7X_SUBLANES_F32 = 8
V7X_SUBLANES_BF16 = 16
V7X_VMEM_BYTES = 64 * 1024 * 1024
V7X_VMEM_LIMIT_BYTES = V7X_VMEM_BYTES - 8 * 1024 * 1024
V7X_VMEM_LIMIT_SCAN_BYTES = V7X_VMEM_BYTES - 4 * 1024 * 1024

SCAN_CHUNK = 256
TOKEN_TILE = 512
WIDE_TILE = 1024
HALO = V7X_SUBLANES_F32
FF_CHUNK = 1024
N_MOD = 6
SCAN_HEADS = 4
LOG2E = 1.4426950408889634
AUG = V7X_SUBLANES_BF16
N_PRE = D_MODEL + A_KV
N_FT = A_KV + 4 * M_HEADS
N_MIX = D_MODEL + D_A + 2 * D_C + N_BRANCH * D_MODEL
MIX_ZM = 0
MIX_Q = MIX_ZM + D_MODEL
MIX_UV = MIX_Q + D_A
MIX_G = MIX_UV + 2 * D_C


def _dot(a, b):
    return jnp.dot(a, b, preferred_element_type=F32)


def _dot_nt(a, b):
    return lax.dot_general(a, b, (((1,), (1,)), ((), ())), preferred_element_type=F32)


def _dot_tn(a, b):
    return lax.dot_general(a, b, (((0,), (0,)), ((), ())), preferred_element_type=F32)


def _sigmoid(x):
    return 0.5 * (1.0 + jnp.tanh(0.5 * x))


def _log_sigmoid(x):
    return jnp.minimum(x, 0.0) - jnp.log(1.0 + jnp.exp(-jnp.abs(x)))


def _gelu_tanh(x):
    return 0.5 * x * (1.0 + jnp.tanh(0.7978845608028654 * (x + 0.044715 * (x * x * x))))


def _layer_norm(x, w, b):
    mu = jnp.mean(x, axis=-1, keepdims=True)
    xc = x - mu
    var = jnp.mean(xc * xc, axis=-1, keepdims=True)
    return xc * lax.rsqrt(var + LN_EPS) * w + b


def _norm_rows(x):
    mu = jnp.mean(x, axis=0, keepdims=True)
    xc = x - mu
    var = jnp.mean(xc * xc, axis=0, keepdims=True)
    return xc * lax.rsqrt(var + LN_EPS)


def _lanes(col, n):
    return jnp.concatenate([col] * (n // V7X_LANES), axis=1)


def _const_spec(shape):
    nd = len(shape)
    return pl.BlockSpec(shape, lambda *_: (0,) * nd, pipeline_mode=pl.Buffered(1))


def _params(n_axes, vmem_limit_bytes=V7X_VMEM_LIMIT_BYTES):
    return pltpu.CompilerParams(dimension_semantics=("arbitrary",) * n_axes, vmem_limit_bytes=vmem_limit_bytes)


def _mod_kernel(c_ref, w_ref, b_ref, o_ref):
    c = c_ref[...]
    s = (c * _sigmoid(c)).astype(BF16)
    o_ref[0] = _dot(s, w_ref[0]) + b_ref[0]


def _modulation(c, ada_w, ada_b):
    n_layers = ada_w.shape[0]
    bsz = c.shape[0]
    mod = pl.pallas_call(
        _mod_kernel,
        out_shape=jax.ShapeDtypeStruct((n_layers * N_MOD, bsz, D_MODEL), F32),
        grid=(n_layers, N_MOD),
        in_specs=[pl.BlockSpec((bsz, D_MODEL), lambda l, j: (0, 0)),
                  pl.BlockSpec((1, D_MODEL, D_MODEL), lambda l, j: (l, 0, j)),
                  pl.BlockSpec((1, 1, D_MODEL), lambda l, j: (l, 0, j))],
        out_specs=pl.BlockSpec((1, bsz, D_MODEL), lambda l, j: (l * N_MOD + j, 0, 0)),
        compiler_params=_params(2),
        name="adaln_mod",
    )(c, ada_w, ada_b)
    return mod.reshape(n_layers * N_MOD, bsz, 1, D_MODEL)


def _mod_spec(layer, k):
    return pl.BlockSpec((None, 1, 1, D_MODEL), lambda b, i: (layer * N_MOD + k, b, 0, 0))


def _pre_kernel(xp_ref, x_ref, xn_ref, sh_ref, sc_ref, w_ref, b_ref, wft_ref, bft_ref, cw_ref, cb_ref,
                wqt_ref, wk_ref, wvt_ref, qt_ref, k_ref, vt_ref, gt_ref, ak_ref, avt_ref, xs_scr, xm_scr):
    t = x_ref.shape[1]
    i = pl.program_id(1)
    last = pl.num_programs(1) - 1
    scale = 1.0 + sc_ref[0]
    shift = sh_ref[0]
    xs_scr[0:HALO, :] = xp_ref[0]
    xs_scr[HALO:HALO + t, :] = x_ref[0]
    xs_scr[HALO + t:, :] = xn_ref[0]
    h_ext = (xs_scr[...] * scale + shift).astype(BF16)
    proj = _dot(h_ext, w_ref[...]) + b_ref[...]
    xm_scr[0:HALO, :] = jnp.where(i > 0, proj[0:HALO, :D_MODEL], 0.0)
    xm_scr[HALO:HALO + t, :] = proj[HALO:HALO + t, :D_MODEL]
    xm_scr[HALO + t:, :] = jnp.where(i < last, proj[HALO + t:, :D_MODEL], 0.0)
    ak_ref[0] = proj[HALO:HALO + t, D_MODEL:D_MODEL + A_KV].astype(BF16)
    h = (x_ref[0] * scale + shift).astype(BF16)
    ft = _dot_nt(wft_ref[...], h) + _lanes(bft_ref[...], t)
    avt_ref[0] = ft[:A_KV].astype(BF16)
    gt_ref[0] = ft[A_KV:]
    conv = cb_ref[...]
    for j in range(CONV_K):
        off = HALO - CONV_K // 2 + j
        conv = conv + cw_ref[j:j + 1, :] * xm_scr[off:off + t, :]
    hc = 0.5 * conv
    xc = (hc + hc * jnp.tanh(hc)).astype(BF16)
    xm = xm_scr[HALO:HALO + t, :].astype(BF16)
    for hd in range(M_HEADS):
        sl = slice(hd * M_DH, (hd + 1) * M_DH)
        qt_ref[0, sl, :] = _dot_nt(wqt_ref[hd], xc[:, sl]).astype(BF16)
        k_ref[0, :, sl] = (_dot(xc[:, sl], wk_ref[hd]) * (M_DH ** -0.5)).astype(BF16)
        vt_ref[0, sl, :] = _dot_nt(wvt_ref[hd], xm[:, sl]).astype(BF16)


def _pre_call(x, mod, layer, w):
    bsz, seq, _ = x.shape
    t = min(WIDE_TILE, seq)
    nt = seq // t
    hb = t // HALO
    nhb = seq // HALO
    tok = lambda n: pl.BlockSpec((1, t, n), lambda b, i: (b, i, 0))
    feat = lambda n: pl.BlockSpec((1, n, t), lambda b, i: (b, 0, i))
    tok_shape = lambda n, dt: jax.ShapeDtypeStruct((bsz, seq, n), dt)
    feat_shape = lambda n, dt: jax.ShapeDtypeStruct((bsz, n, seq), dt)
    return pl.pallas_call(
        _pre_kernel,
        out_shape=(feat_shape(D_MODEL, BF16), tok_shape(D_MODEL, BF16), feat_shape(D_MODEL, BF16),
                   feat_shape(4 * M_HEADS, F32), tok_shape(A_KV, BF16), feat_shape(A_KV, BF16)),
        grid=(bsz, nt),
        in_specs=[pl.BlockSpec((1, HALO, D_MODEL), lambda b, i: (b, jnp.maximum(i * hb - 1, 0), 0)),
                  tok(D_MODEL),
                  pl.BlockSpec((1, HALO, D_MODEL), lambda b, i: (b, jnp.minimum((i + 1) * hb, nhb - 1), 0)),
                  _mod_spec(layer, 0), _mod_spec(layer, 1),
                  _const_spec((D_MODEL, N_PRE)), _const_spec((1, N_PRE)),
                  _const_spec((N_FT, D_MODEL)), _const_spec((N_FT, V7X_LANES)),
                  _const_spec((CONV_K, D_MODEL)), _const_spec((1, D_MODEL)),
                  _const_spec((M_HEADS, M_DH, M_DH)), _const_spec((M_HEADS, M_DH, M_DH)),
                  _const_spec((M_HEADS, M_DH, M_DH))],
        out_specs=(feat(D_MODEL), tok(D_MODEL), feat(D_MODEL), feat(4 * M_HEADS), tok(A_KV), feat(A_KV)),
        scratch_shapes=[pltpu.VMEM((t + 2 * HALO, D_MODEL), F32), pltpu.VMEM((t + 2 * HALO, D_MODEL), F32)],
        compiler_params=_params(2),
        name="pre_proj_conv_qkv",
    )(x, x, x, mod, mod, w["w_pre"], w["b_pre"], w["w_ft"], w["b_ft"], w["conv_w"], w["conv_b"],
      w["wqt"], w["wk"], w["wvt"])


def _scan_kernel(qt_ref, k_ref, vt_ref, gr_ref, o_ref, hs_scr, c_scr, sc_scr):
    seq = k_ref.shape[1]
    lc = gr_ref.shape[4]
    nc = seq // lc
    nh = gr_ref.shape[2]
    si = lax.broadcasted_iota(jnp.int32, (lc, lc), 0)
    li = lax.broadcasted_iota(jnp.int32, (lc, lc), 1)
    masks = (si <= li, si >= li)
    ones_rows = jnp.ones((AUG, lc), BF16)
    c_scr[...] = jnp.zeros_like(c_scr)

    gate = {}
    for hd in range(nh):
        for d in range(2):
            ig = gr_ref[0, 2 * d, hd] * LOG2E
            lf = _log_sigmoid(gr_ref[0, 2 * d + 1, hd]) * LOG2E
            a = jnp.dot(lf, masks[d].astype(F32), precision=lax.Precision.HIGHEST, preferred_element_type=F32)
            w = ig - a
            a_tot = a[:, lc - 1:lc] if d == 0 else a[:, 0:1]
            w_max = jnp.max(w, axis=1, keepdims=True)
            w_cols = jnp.concatenate([w, jnp.zeros((V7X_LANES - nc, lc), F32)], axis=0).T
            m = jnp.zeros((1, 1), F32)
            m_seq = []
            for step in range(nc):
                jc = step if d == 0 else nc - 1 - step
                m_next = a_tot[jc:jc + 1] + jnp.maximum(m, w_max[jc:jc + 1])
                m_seq.append((m, m_next))
                m = m_next
            gate[hd, d] = (a, w, a_tot, w_cols, m_seq)

    for hd in range(nh):
        rows = slice(hd * M_DH, (hd + 1) * M_DH)
        for jc in range(nc):
            cols = slice(jc * lc, (jc + 1) * lc)
            sc_scr[hd, jc] = _dot(k_ref[0, cols, rows], qt_ref[0, rows, cols]).astype(BF16)

    for step in range(nc):
        for hd in range(nh):
            rows = slice(hd * M_DH, (hd + 1) * M_DH)
            for d in range(2):
                a, w, a_tot, w_cols, m_seq = gate[hd, d]
                jc = step if d == 0 else nc - 1 - step
                cols = slice(jc * lc, (jc + 1) * lc)
                m, m_next = m_seq[step]
                qt = qt_ref[0, rows, cols]
                k = k_ref[0, cols, rows]
                vta = jnp.concatenate([vt_ref[0, rows, cols], ones_rows], axis=0)
                wm = jnp.where(masks[d], w_cols[:, jc:jc + 1], -jnp.inf)
                u = -jnp.maximum(m, jnp.max(wm, axis=0, keepdims=True))
                st = sc_scr[hd, jc] * jnp.exp2(wm + u).astype(BF16)
                big = (_dot(vta, st)
                       + _dot(c_scr[hd, d].astype(BF16), qt * jnp.exp2(m + u).astype(BF16)))
                den = jnp.maximum(jnp.abs(big[M_DH:M_DH + 1, :]), jnp.exp2(u - a[jc:jc + 1, :]))
                ht = big[:M_DH, :] * (1.0 / den)
                if step < nc // 2:
                    hs_scr[rows, cols] = ht
                else:
                    hs_scr[rows, cols] = hs_scr[rows, cols] + ht
                wg = jnp.exp2(a_tot[jc:jc + 1] + w[jc:jc + 1, :] - m_next)
                decay = jnp.exp2(a_tot[jc:jc + 1] + m - m_next)
                c_scr[hd, d] = decay * c_scr[hd, d] + _dot(vta * wg.astype(BF16), k)

    blk = min(512, seq)
    for hd in range(nh):
        rows = slice(hd * M_DH, (hd + 1) * M_DH)
        for r in range(seq // blk):
            cols = slice(r * blk, (r + 1) * blk)
            o_ref[0, rows, cols] = _norm_rows(hs_scr[rows, cols]).astype(o_ref.dtype)


def _scan_call(qt, k, vt, grow):
    bsz, seq, _ = k.shape
    nc, lc = grow.shape[3], grow.shape[4]
    nh = SCAN_HEADS
    feat = pl.BlockSpec((1, nh * M_DH, seq), lambda b, h: (b, h, 0))
    return pl.pallas_call(
        _scan_kernel,
        out_shape=jax.ShapeDtypeStruct((bsz, D_MODEL, seq), BF16),
        grid=(bsz, M_HEADS // nh),
        in_specs=[feat, pl.BlockSpec((1, seq, nh * M_DH), lambda b, h: (b, 0, h)), feat,
                  pl.BlockSpec((1, 4, nh, nc, lc), lambda b, h: (b, 0, h, 0, 0))],
        out_specs=feat,
        scratch_shapes=[pltpu.VMEM((nh * M_DH, seq), F32), pltpu.VMEM((nh, 2, M_DH + AUG, M_DH), F32),
                        pltpu.VMEM((nh, nc, lc, lc), BF16)],
        compiler_params=_params(2, V7X_VMEM_LIMIT_SCAN_BYTES),
        name="mlstm_scan",
    )(qt, k, vt, grow)


def _mix_kernel(x_ref, hnt_ref, kp_ref, k_ref, kn_ref, vtp_ref, vt_ref, vtn_ref, sh_ref, sc_ref, gm_ref, sink_ref,
                wt_ref, bt_ref, nw_ref, clw_ref, clb_ref, ws_ref, bs_ref, pmt_ref, pat_ref, pct_ref, wo_ref,
                l1w_ref, l1b_ref, o_ref, kx_scr, vtx_scr, yat_scr, yct_scr, s_scr):
    t = x_ref.shape[1]
    i = pl.program_id(1)
    last = pl.num_programs(1) - 1
    nqb = t // WINDOW
    x = x_ref[0]
    h = (x * (1.0 + sc_ref[0]) + sh_ref[0]).astype(BF16)

    def proj_t(off, n):
        return _dot_nt(wt_ref[off:off + n, :], h) + _lanes(bt_ref[off:off + n, :], t)

    p_zm = proj_t(MIX_ZM, D_MODEL)
    p_q = proj_t(MIX_Q, D_A)
    p_uv = proj_t(MIX_UV, 2 * D_C)
    p_g0 = proj_t(MIX_G, D_MODEL)
    p_g1 = proj_t(MIX_G + D_MODEL, D_MODEL)
    p_g2 = proj_t(MIX_G + 2 * D_MODEL, D_MODEL)

    y_mt = (hnt_ref[0].astype(F32) * _lanes(nw_ref[...], t) * _sigmoid(p_zm)).astype(BF16)
    acc = _sigmoid(p_g0) * _dot(pmt_ref[...], y_mt)

    kx_scr[0:WINDOW, :] = kp_ref[0]
    kx_scr[WINDOW:WINDOW + t, :] = k_ref[0]
    kx_scr[WINDOW + t:, :] = kn_ref[0]
    vtx_scr[:, 0:WINDOW] = vtp_ref[0]
    vtx_scr[:, WINDOW:WINDOW + t] = vt_ref[0]
    vtx_scr[:, WINDOW + t:] = vtn_ref[0]
    qt = (p_q * (LOG2E * A_DH ** -0.5)).astype(BF16)
    kj = lax.broadcasted_iota(jnp.int32, (3 * WINDOW, WINDOW), 0)
    qi = lax.broadcasted_iota(jnp.int32, (3 * WINDOW, WINDOW), 1)
    dist = jnp.abs(qi + WINDOW - kj)
    band = dist <= WINDOW
    distf = dist.astype(F32)
    first_pen = jnp.where(i == 0, -jnp.inf, 0.0)
    last_pen = jnp.where(i == last, -jnp.inf, 0.0)
    ones_rows = jnp.ones((AUG, 3 * WINDOW), BF16)
    for g in range(A_KV_HEADS):
        for n in range(nqb):
            qb = jnp.concatenate([qt[(g * A_REP + r) * A_DH:(g * A_REP + r + 1) * A_DH, n * WINDOW:(n + 1) * WINDOW]
                                  for r in range(A_REP)], axis=1)
            kb = kx_scr[n * WINDOW:(n + 3) * WINDOW, g * A_DH:(g + 1) * A_DH]
            s_scr[g * nqb + n] = _dot(kb, qb)
    for g in range(A_KV_HEADS):
        bias = jnp.concatenate(
            [jnp.where(band, -(LOG2E * 2.0 ** (-8.0 * (g * A_REP + r + 1) / A_HEADS)) * distf, -jnp.inf)
             for r in range(A_REP)], axis=1)
        sink = jnp.concatenate([jnp.full((1, WINDOW), sink_ref[g * A_REP + r] * LOG2E, F32) for r in range(A_REP)],
                               axis=1)
        for n in range(nqb):
            vtb = jnp.concatenate([vtx_scr[g * A_DH:(g + 1) * A_DH, n * WINDOW:(n + 3) * WINDOW], ones_rows], axis=0)
            blk_bias = bias
            if n == 0:
                blk_bias = jnp.concatenate([bias[:WINDOW] + first_pen, bias[WINDOW:]], axis=0)
            if n == nqb - 1:
                blk_bias = jnp.concatenate([blk_bias[:2 * WINDOW], blk_bias[2 * WINDOW:] + last_pen], axis=0)
            s = s_scr[g * nqb + n] + blk_bias
            mx = jnp.maximum(jnp.max(s, axis=0, keepdims=True), sink)
            p = jnp.exp2(s - mx).astype(BF16)
            ot = _dot(vtb, p)
            ot = ot[:A_DH] * (1.0 / (ot[A_DH:A_DH + 1] + jnp.exp2(sink - mx)))
            for r in range(A_REP):
                r0 = (g * A_REP + r) * A_DH
                yat_scr[r0:r0 + A_DH, n * WINDOW:(n + 1) * WINDOW] = ot[:, r * WINDOW:(r + 1) * WINDOW]
    acc = acc + _sigmoid(p_g1) * _dot(pat_ref[...], yat_scr[...].astype(BF16))

    uvt = _gelu_tanh(p_uv)
    vnt = (_norm_rows(uvt[D_C:]) * _lanes(clw_ref[...], t) + _lanes(clb_ref[...], t)).astype(BF16)
    gw = D_C // C_GROUPS
    for n in range(t // C_CHUNK):
        for g in range(C_GROUPS):
            vst = _dot_nt(vnt[g * gw:(g + 1) * gw, n * C_CHUNK:(n + 1) * C_CHUNK], ws_ref[g]) + bs_ref[g:g + 1, :]
            yct_scr[g * gw:(g + 1) * gw, n * C_CHUNK:(n + 1) * C_CHUNK] = vst
    y_ct = (uvt[:D_C] * yct_scr[...]).astype(BF16)
    acc = acc + _sigmoid(p_g2) * _dot(pct_ref[...], y_ct)

    mix = _dot_tn(acc.astype(BF16), wo_ref[...])
    o_ref[0] = _layer_norm(ALPHA * x + (1.0 + gm_ref[0]) * mix, l1w_ref[...], l1b_ref[...])


def _mix_call(x, hnt, ak, avt, mod, layer, w):
    bsz, seq, _ = x.shape
    t = min(TOKEN_TILE, seq)
    nt = seq // t
    wb = t // WINDOW
    nwb = seq // WINDOW
    tok = lambda n: pl.BlockSpec((1, t, n), lambda b, i: (b, i, 0))
    feat = lambda n: pl.BlockSpec((1, n, t), lambda b, i: (b, 0, i))
    prev_blk = lambda b, i: jnp.maximum(i * wb - 1, 0)
    next_blk = lambda b, i: jnp.minimum((i + 1) * wb, nwb - 1)
    return pl.pallas_call(
        _mix_kernel,
        out_shape=jax.ShapeDtypeStruct((bsz, seq, D_MODEL), F32),
        grid=(bsz, nt),
        in_specs=[tok(D_MODEL), feat(D_MODEL),
                  pl.BlockSpec((1, WINDOW, A_KV), lambda b, i: (b, prev_blk(b, i), 0)),
                  tok(A_KV),
                  pl.BlockSpec((1, WINDOW, A_KV), lambda b, i: (b, next_blk(b, i), 0)),
                  pl.BlockSpec((1, A_KV, WINDOW), lambda b, i: (b, 0, prev_blk(b, i))),
                  feat(A_KV),
                  pl.BlockSpec((1, A_KV, WINDOW), lambda b, i: (b, 0, next_blk(b, i))),
                  _mod_spec(layer, 0), _mod_spec(layer, 1), _mod_spec(layer, 2),
                  pl.BlockSpec(memory_space=pltpu.SMEM),
                  _const_spec((N_MIX, D_MODEL)), _const_spec((N_MIX, V7X_LANES)),
                  _const_spec((D_MODEL, V7X_LANES)), _const_spec((D_C, V7X_LANES)), _const_spec((D_C, V7X_LANES)),
                  _const_spec((C_GROUPS, C_CHUNK, C_CHUNK)), _const_spec((C_GROUPS, C_CHUNK)),
                  _const_spec((D_MODEL, D_MODEL)), _const_spec((D_MODEL, D_A)), _const_spec((D_MODEL, D_C)),
                  _const_spec((D_MODEL, D_MODEL)), _const_spec((1, D_MODEL)), _const_spec((1, D_MODEL))],
        out_specs=tok(D_MODEL),
        scratch_shapes=[pltpu.VMEM((t + 2 * WINDOW, A_KV), BF16), pltpu.VMEM((A_KV, t + 2 * WINDOW), BF16),
                        pltpu.VMEM((D_A, t), F32), pltpu.VMEM((D_C, t), F32),
                        pltpu.VMEM((A_KV_HEADS * (t // WINDOW), 3 * WINDOW, A_REP * WINDOW), F32)],
        compiler_params=_params(2),
        name="mix_merge_ln1",
    )(x, hnt, ak, ak, ak, avt, avt, avt, mod, mod, mod, w["sink"], w["w_mixt"], w["b_mixt"], w["norm_w"],
      w["c_ln_w"], w["c_ln_b"], w["c_ws"], w["c_bs"], w["p_mt"], w["p_at"], w["p_ct"], w["w_out"],
      w["ln1_w"], w["ln1_b"])


def _mlp_kernel(x_ref, sh_ref, sc_ref, gm_ref, w1_ref, b1_ref, w2_ref, b2_ref, lw_ref, lb_ref, o_ref):
    x = x_ref[0]
    h = (x * (1.0 + sc_ref[0]) + sh_ref[0]).astype(BF16)
    ff = b2_ref[...]
    for c in range(D_FF // FF_CHUNK):
        cs = slice(c * FF_CHUNK, (c + 1) * FF_CHUNK)
        hid = jnp.maximum(_dot(h, w1_ref[:, cs]) + b1_ref[:, cs], 0.0)
        ff = ff + _dot((hid * hid).astype(BF16), w2_ref[cs, :])
    o_ref[0] = _layer_norm(ALPHA * x + (1.0 + gm_ref[0]) * ff, lw_ref[...], lb_ref[...])


def _mlp_call(x, mod, layer, w):
    bsz, seq, _ = x.shape
    t = min(WIDE_TILE, seq)
    tile = pl.BlockSpec((1, t, D_MODEL), lambda b, i: (b, i, 0))
    return pl.pallas_call(
        _mlp_kernel,
        out_shape=jax.ShapeDtypeStruct((bsz, seq, D_MODEL), F32),
        grid=(bsz, seq // t),
        in_specs=[tile, _mod_spec(layer, 3), _mod_spec(layer, 4), _mod_spec(layer, 5),
                  _const_spec((D_MODEL, D_FF)), _const_spec((1, D_FF)),
                  _const_spec((D_FF, D_MODEL)), _const_spec((1, D_MODEL)),
                  _const_spec((1, D_MODEL)), _const_spec((1, D_MODEL))],
        out_specs=tile,
        compiler_params=_params(2),
        name="mlp_ln2",
    )(x, mod, mod, mod, w["w1"], w["b1"], w["w2"], w["b2"], w["ln2_w"], w["ln2_b"])


def _layer_weights(p, l):
    w_in, b_in = p["w_in"][l], p["b_in"][l]
    row = lambda name: p[name][l][None, :]
    col = lambda v: jnp.broadcast_to(v[:, None], (v.shape[0], V7X_LANES))
    w_pre = jnp.concatenate([w_in[:, OFF_MX:OFF_MZ], w_in[:, OFF_AK:OFF_AV]], axis=1)
    b_pre = jnp.concatenate([b_in[OFF_MX:OFF_MZ], b_in[OFF_AK:OFF_AV]])
    w_ft = jnp.concatenate([w_in[:, OFF_AV:OFF_C], w_in[:, OFF_MG:OFF_AQ]], axis=1)
    b_ft = jnp.concatenate([b_in[OFF_AV:OFF_C], b_in[OFF_MG:OFF_AQ]])
    w_mix = jnp.concatenate([w_in[:, OFF_MZ:OFF_MG], w_in[:, OFF_AQ:OFF_AK], w_in[:, OFF_C:]], axis=1)
    b_mix = jnp.concatenate([b_in[OFF_MZ:OFF_MG], b_in[OFF_AQ:OFF_AK], b_in[OFF_C:]])
    return dict(
        w_pre=w_pre.astype(BF16), b_pre=b_pre[None, :],
        w_ft=w_ft.T.astype(BF16), b_ft=col(b_ft),
        w_mixt=w_mix.T.astype(BF16), b_mixt=col(b_mix),
        conv_w=p["m_conv_w"][l], conv_b=row("m_conv_b"),
        wqt=jnp.transpose(p["m_wq"][l], (0, 2, 1)).astype(BF16), wk=p["m_wk"][l].astype(BF16),
        wvt=jnp.transpose(p["m_wv"][l], (0, 2, 1)).astype(BF16),
        norm_w=col(p["m_norm_w"][l]), sink=p["a_sink"][l],
        c_ln_w=col(p["c_ln_w"][l]), c_ln_b=col(p["c_ln_b"][l]),
        c_ws=p["c_ws"][l].astype(BF16), c_bs=p["c_bs"][l],
        p_mt=p["p_m"][l].T.astype(BF16), p_at=p["p_a"][l].T.astype(BF16), p_ct=p["p_c"][l].T.astype(BF16),
        w_out=p["w_out"][l].astype(BF16), ln1_w=row("ln1_w"), ln1_b=row("ln1_b"),
        w1=p["mlp_w1"][l].astype(BF16), b1=row("mlp_b1"), w2=p["mlp_w2"][l].astype(BF16), b2=row("mlp_b2"),
        ln2_w=row("ln2_w"), ln2_b=row("ln2_b"))


def _trunk(x, c, ada_w, ada_b, weights):
    bsz, seq, _ = x.shape
    lc = min(SCAN_CHUNK, seq)
    nc = seq // lc
    mod = _modulation(c, ada_w, ada_b)
    for l in range(DEPTH):
        w = weights[l]
        qt, k, vt, gt, ak, avt = _pre_call(x, mod, l, w)
        hnt = _scan_call(qt, k, vt, gt.reshape(bsz, 4, M_HEADS, nc, lc))
        x = _mix_call(x, hnt, ak, avt, mod, l, w)
        x = _mlp_call(x, mod, l, w)
    return x


def kernel(x_prompt, x_sample, c_prompt, c_sample, ada_w, ada_b, w_in, b_in, m_conv_w, m_conv_b, m_wq, m_wk, m_wv,
           m_norm_w, a_sink, c_ln_w, c_ln_b, c_ws, c_bs, p_m, p_a, p_c, w_out, ln1_w, ln1_b, mlp_w1, mlp_b1,
           mlp_w2, mlp_b2, ln2_w, ln2_b):
    p = dict(w_in=w_in, b_in=b_in, m_conv_w=m_conv_w, m_conv_b=m_conv_b, m_wq=m_wq,
             m_wk=m_wk, m_wv=m_wv, m_norm_w=m_norm_w, a_sink=a_sink, c_ln_w=c_ln_w, c_ln_b=c_ln_b, c_ws=c_ws,
             c_bs=c_bs, p_m=p_m, p_a=p_a, p_c=p_c, w_out=w_out, ln1_w=ln1_w, ln1_b=ln1_b, mlp_w1=mlp_w1,
             mlp_b1=mlp_b1, mlp_w2=mlp_w2, mlp_b2=mlp_b2, ln2_w=ln2_w, ln2_b=ln2_b)
    weights = [_layer_weights(p, l) for l in range(DEPTH)]
    ada_wb = ada_w.astype(BF16)
    ada_b3 = ada_b.reshape(DEPTH, 1, -1)
    return (_trunk(x_prompt, c_prompt, ada_wb, ada_b3, weights), _trunk(x_sample, c_sample, ada_wb, ada_b3, weights))
```
